```python
import math
import jax, jax.numpy as jnp
from jax import lax
import numpy as np

D_MODEL = 4096
BATCH = 4
SEQ = 4096
DEPTH = 2
DEC_BATCH = 16
DEC_SEQ = 64
PAST_LEN = 4096

CHUNK = 64
N_EVEN = (DEPTH + 1) // 2
N_ODD = DEPTH // 2
EPS = 1e-6

SSD_WIDTH = D_MODEL // 2
SSD_HEADDIM = 64
SSD_HEADS = SSD_WIDTH // SSD_HEADDIM
SSD_GROUPS = 8
SSD_RPG = SSD_HEADS // SSD_GROUPS
SSD_STATE = 128
CONV_W = 4
CONV_CH = SSD_WIDTH + 2 * SSD_GROUPS * SSD_STATE

MLA_HEADS = 16
QK_NOPE = 128
QK_ROPE = 64
V_DIM = 128
Q_LORA = 1024
KV_LORA = 512
ROPE_THETA = 10000.0
ATTN_BLOCK = 128
MLA_SCALE = (QK_NOPE + QK_ROPE) ** -0.5

OFF_XBC = SSD_WIDTH
OFF_DT = OFF_XBC + CONV_CH
OFF_CQ = OFF_DT + SSD_HEADS
OFF_CKV = OFF_CQ + Q_LORA
OFF_KPE = OFF_CKV + KV_LORA
IN0_WIDTH = OFF_KPE + QK_ROPE
MIX0_WIDTH = SSD_WIDTH + MLA_HEADS * V_DIM

RET_HEADS = 16
RET_KDIM = D_MODEL // RET_HEADS
RET_VDIM = 2 * RET_KDIM
RET_QK = RET_HEADS * RET_KDIM
RET_V = RET_HEADS * RET_VDIM
IN1_WIDTH = 2 * RET_QK + 2 * RET_V

D_FF = -(-8 * D_MODEL // (3 * 256)) * 256

kernel_name = 'hybrid_ssd_mla_retention_stream_step'


def rms_norm(x, g=None):
    xf = x.astype(jnp.float32)
    y = xf * lax.rsqrt(jnp.mean(xf * xf, axis=-1, keepdims=True) + EPS)
    if g is not None:
        y = y * g.astype(jnp.float32)
    return y.astype(x.dtype)


def rope(x, pos):
    half = x.shape[-1] // 2
    inv = ROPE_THETA ** (-jnp.arange(half, dtype=jnp.float32) / half)
    ang = pos.astype(jnp.float32)[:, None] * inv[None, :]
    shape = (1, pos.shape[0]) + (1,) * (x.ndim - 3) + (half,)
    cos = jnp.cos(ang).reshape(shape).astype(x.dtype)
    sin = jnp.sin(ang).reshape(shape).astype(x.dtype)
    x1, x2 = x[..., :half], x[..., half:]
    return jnp.concatenate([x1 * cos - x2 * sin, x1 * sin + x2 * cos], axis=-1)


def ada_modulation(c, w, b):
    m = (jax.nn.silu(c) @ w + b)[:, None, :]
    return jnp.split(m, 6, axis=-1)


def causal_conv(u, buf, w, b):
    up = jnp.concatenate([buf.astype(u.dtype), u], axis=1)
    y = lax.conv_general_dilated(up, w[:, None, :].astype(u.dtype), window_strides=(1,), padding='VALID',
                                 dimension_numbers=('NWC', 'WIO', 'NWC'), feature_group_count=u.shape[-1])
    return jax.nn.silu(y + b), up[:, -(CONV_W - 1):]


def ssd_scan(x, dt, a, bm, cm, h0):
    b, T = x.shape[:2]
    L = min(CHUNK, T)
    nc = T // L
    idx = jnp.arange(L)
    causal = (idx[:, None] >= idx[None, :])[None, :, :, None, None]

    def to_chunks(t):
        return jnp.moveaxis(t.reshape((b, nc, L) + t.shape[2:]), 1, 0)

    def step(h, inp):
        xc, dtc, bc, cc = inp
        cs = jnp.cumsum(dtc * a, axis=1)
        seg = cs[:, :, None] - cs[:, None, :]
        decay = jnp.exp(jnp.where(causal, seg, -jnp.inf))
        cb = jnp.einsum('bign,bjgn->bijg', cc, bc)
        wts = cb[..., None] * decay * dtc[:, None]
        y = jnp.einsum('bijgr,bjgrp->bigrp', wts, xc)
        y = y + jnp.einsum('bign,bgrpn->bigrp', cc, h) * jnp.exp(cs)[..., None]
        tail = jnp.exp(cs[:, -1:] - cs) * dtc
        h = h * jnp.exp(cs[:, -1])[..., None, None] + jnp.einsum('bjgr,bjgrp,bjgn->bgrpn', tail, xc, bc)
        return h, y

    h, ys = lax.scan(step, h0, (to_chunks(x), to_chunks(dt), to_chunks(bm), to_chunks(cm)))
    return jnp.moveaxis(ys, 0, 1).reshape(x.shape), h


def ssd_mixer(z, xbc, dt_raw, conv_buf, h0, P, j):
    b, T, _ = z.shape
    f32 = jnp.float32
    xbc, new_buf = causal_conv(xbc, conv_buf, P['conv_w'][j], P['conv_b'][j])
    xs, bm, cm = jnp.split(xbc, [SSD_WIDTH, SSD_WIDTH + SSD_GROUPS * SSD_STATE], axis=-1)
    x = xs.reshape(b, T, SSD_GROUPS, SSD_RPG, SSD_HEADDIM).astype(f32)
    bm = bm.reshape(b, T, SSD_GROUPS, SSD_STATE).astype(f32)
    cm = cm.reshape(b, T, SSD_GROUPS, SSD_STATE).astype(f32)
    dt = jax.nn.softplus(dt_raw.astype(f32) + P['dt_bias'][j].astype(f32)).reshape(b, T, SSD_GROUPS, SSD_RPG)
    a = -jnp.exp(P['a_log'][j].astype(f32)).reshape(SSD_GROUPS, SSD_RPG)
    h0 = h0.astype(f32).reshape(b, SSD_GROUPS, SSD_RPG, SSD_HEADDIM, SSD_STATE)
    y, h = ssd_scan(x, dt, a, bm, cm, h0)
    y = y + x * P['d_skip'][j].astype(f32).reshape(SSD_GROUPS, SSD_RPG, 1)
    y = y.reshape(b, T, SSD_WIDTH).astype(z.dtype) * jax.nn.silu(z)
    y = rms_norm(y.reshape(b, T, SSD_GROUPS, SSD_WIDTH // SSD_GROUPS),
                 P['ssd_norm'][j].reshape(SSD_GROUPS, SSD_WIDTH // SSD_GROUPS)).reshape(b, T, SSD_WIDTH)
    return y, new_buf, h.reshape(b, SSD_HEADS, SSD_HEADDIM, SSD_STATE)


def mla_mixer(cq, ckv_raw, kpe_raw, pos, past_ckv, past_kpe, P, j):
    b, T, _ = cq.shape
    q = (rms_norm(cq, P['q_a_norm'][j]) @ P['w_uq'][j]).reshape(b, T, MLA_HEADS, QK_NOPE + QK_ROPE)
    q_nope = rms_norm(q[..., :QK_NOPE], P['q_norm_nope'][j])
    q_pe = rope(rms_norm(q[..., QK_NOPE:], P['q_norm_rope'][j]), pos)
    ckv = rms_norm(ckv_raw, P['kv_a_norm'][j])
    kpe = rope(rms_norm(kpe_raw, P['k_norm_rope'][j]), pos)
    if past_ckv is None:
        all_ckv, all_kpe, k_pos = ckv, kpe, pos
    else:
        all_ckv = jnp.concatenate([past_ckv.astype(ckv.dtype), ckv], axis=1)
        all_kpe = jnp.concatenate([past_kpe.astype(kpe.dtype), kpe], axis=1)
        k_pos = jnp.arange(all_ckv.shape[1])
    kv = (all_ckv @ P['w_ukv'][j]).reshape(b, all_ckv.shape[1], MLA_HEADS, QK_NOPE + V_DIM)
    k_nope = rms_norm(kv[..., :QK_NOPE], P['k_norm_nope'][j])
    v = kv[..., QK_NOPE:]
    blk = min(ATTN_BLOCK, T)
    nb = T // blk

    def to_blocks(t):
        return jnp.swapaxes(t.reshape((b, nb, blk) + t.shape[2:]), 0, 1)

    def attend(args):
        qn, qp, qpos = args
        s = jnp.einsum('bqhd,bkhd->bhqk', qn, k_nope) + jnp.einsum('bqhr,bkr->bhqk', qp, all_kpe)
        s = s.astype(jnp.float32) * MLA_SCALE
        visible = k_pos[None, :] < (qpos[:, None] // CHUNK + 1) * CHUNK
        p = jax.nn.softmax(jnp.where(visible, s, -jnp.inf), axis=-1).astype(v.dtype)
        return jnp.einsum('bhqk,bkhd->bqhd', p, v)

    o = lax.map(attend, (to_blocks(q_nope), to_blocks(q_pe), pos.reshape(nb, blk)))
    o = jnp.swapaxes(o, 0, 1).reshape(b, T, MLA_HEADS * V_DIM)
    return o, ckv, kpe


def even_mixer(h, pos, j, caches, P):
    b = h.shape[0]
    proj = h @ P['w_in0'][j]
    z, xbc, dt_raw, cq, ckv_raw, kpe_raw = jnp.split(proj, [OFF_XBC, OFF_DT, OFF_CQ, OFF_CKV, OFF_KPE], axis=-1)
    if caches is None:
        conv_buf = jnp.zeros((b, CONV_W - 1, CONV_CH), h.dtype)
        h0 = jnp.zeros((b, SSD_HEADS, SSD_HEADDIM, SSD_STATE), jnp.float32)
        past_ckv, past_kpe = None, None
    else:
        conv_buf, h0 = caches['conv'][j], caches['ssd'][j]
        past_ckv, past_kpe = caches['ckv'][j], caches['kpe'][j]
    y_ssd, new_buf, h_new = ssd_mixer(z, xbc, dt_raw, conv_buf, h0, P, j)
    y_mla, ckv, kpe = mla_mixer(cq, ckv_raw, kpe_raw, pos, past_ckv, past_kpe, P, j)
    out = jnp.concatenate([y_ssd, y_mla.astype(y_ssd.dtype)], axis=-1) @ P['w_out0'][j]
    return out, ckv, kpe, h_new, new_buf


def retention_scan(q, k, v, s0):
    b, T = q.shape[:2]
    L = min(CHUNK, T)
    nc = T // L
    lg = jnp.log1p(-jnp.exp2(-5.0 - jnp.arange(RET_HEADS, dtype=jnp.float32)))
    idx = jnp.arange(L, dtype=jnp.float32)
    rel = (idx[:, None] - idx[None, :])[..., None]
    dmask = jnp.exp(jnp.where(rel >= 0, rel * lg, -jnp.inf))
    q_decay = jnp.exp((idx[:, None] + 1.0) * lg)[:, :, None]
    k_decay = jnp.exp((L - 1.0 - idx[:, None]) * lg)[:, :, None]
    c_decay = jnp.exp(L * lg)[:, None, None]

    def to_chunks(t):
        return jnp.moveaxis(t.reshape((b, nc, L) + t.shape[2:]), 1, 0)

    def step(s, inp):
        qc, kc, vc = inp
        att = jnp.einsum('bihd,bjhd->bijh', qc, kc) * dmask
        o = jnp.einsum('bijh,bjhv->bihv', att, vc) + jnp.einsum('bihd,bhdv->bihv', qc * q_decay, s)
        s = s * c_decay + jnp.einsum('bjhd,bjhv->bhdv', kc * k_decay, vc)
        return s, o

    s, o = lax.scan(step, s0, (to_chunks(q), to_chunks(k), to_chunks(v)))
    return jnp.moveaxis(o, 0, 1).reshape(b, T, RET_HEADS, RET_VDIM), s


def retention_mixer(h, pos, s0, w_in, w_out):
    b, T, _ = h.shape
    f32 = jnp.float32
    q, k, v, g = jnp.split(h @ w_in, [RET_QK, 2 * RET_QK, 2 * RET_QK + RET_V], axis=-1)
    q = rope(q.reshape(b, T, RET_HEADS, RET_KDIM), pos)
    k = rope(k.reshape(b, T, RET_HEADS, RET_KDIM), pos) * (RET_KDIM ** -0.5)
    v = v.reshape(b, T, RET_HEADS, RET_VDIM)
    o, s = retention_scan(q.astype(f32), k.astype(f32), v.astype(f32), s0.astype(f32))
    mu = jnp.mean(o, axis=-1, keepdims=True)
    var = jnp.mean(jnp.square(o - mu), axis=-1, keepdims=True)
    o = ((o - mu) * lax.rsqrt(var + EPS)).reshape(b, T, RET_V).astype(h.dtype)
    return (jax.nn.silu(g) * o) @ w_out, s


def swiglu(h, wg, wu, wd):
    return (jax.nn.silu(h @ wg) * (h @ wu)) @ wd


def run_trunk(x, c, pos, caches, P):
    st = {'ckv': [], 'kpe': [], 'ssd': [], 'conv': [], 'ret': []}
    for i in range(DEPTH):
        j = i // 2
        sh_m, sc_m, g_m, sh_f, sc_f, g_f = ada_modulation(c, P['w_ada'][i], P['b_ada'][i])
        hn = rms_norm(x, P['norm_mix'][i]) * (1 + sc_m) + sh_m
        if i % 2 == 0:
            mix, ckv, kpe, hs, buf = even_mixer(hn, pos, j, caches, P)
            st['ckv'].append(ckv)
            st['kpe'].append(kpe)
            st['ssd'].append(hs)
            st['conv'].append(buf)
        else:
            if caches is None:
                s0 = jnp.zeros((x.shape[0], RET_HEADS, RET_KDIM, RET_VDIM), jnp.float32)
            else:
                s0 = caches['ret'][j]
            mix, s = retention_mixer(hn, pos, s0, P['w_in1'][j], P['w_out1'][j])
            st['ret'].append(s)
        x = x + g_m * mix
        hn = rms_norm(x, P['norm_ffn'][i]) * (1 + sc_f) + sh_f
        x = x + g_f * swiglu(hn, P['w_gate'][i], P['w_up'][i], P['w_down'][i])
    return x, jnp.stack(st['ckv']), jnp.stack(st['kpe']), jnp.stack(st['ssd']), jnp.stack(st['conv']), jnp.stack(st['ret'])


def setup_inputs(seed: int = 0) -> dict:
    key = jax.random.key(seed)
    ks = iter(jax.random.split(key, 48))
    f32 = jnp.float32

    def nrm(shape, scale=1.0):
        return jax.random.normal(next(ks), shape, f32) * scale

    def gain(shape):
        return 1.0 + 0.02 * jax.random.normal(next(ks), shape, f32)

    dt0 = jnp.exp(jax.random.uniform(next(ks), (N_EVEN, SSD_HEADS), f32, math.log(1e-3), math.log(1e-1)))
    return {
        'x_prompt': nrm((BATCH, SEQ, D_MODEL)),
        'x_sample': nrm((DEC_BATCH, DEC_SEQ, D_MODEL)),
        'c_prompt': nrm((BATCH, D_MODEL)),
        'c_sample': nrm((DEC_BATCH, D_MODEL)),
        'cache_mla_ckv': nrm((N_EVEN, DEC_BATCH, PAST_LEN, KV_LORA)),
        'cache_mla_kpe': nrm((N_EVEN, DEC_BATCH, PAST_LEN, QK_ROPE)),
        'state_ssd': nrm((N_EVEN, DEC_BATCH, SSD_HEADS, SSD_HEADDIM, SSD_STATE), 0.1),
        'state_ssd_conv': nrm((N_EVEN, DEC_BATCH, CONV_W - 1, CONV_CH)),
        'state_ret': nrm((N_ODD, DEC_BATCH, RET_HEADS, RET_KDIM, RET_VDIM), 0.5),
        'norm_mix': gain((DEPTH, D_MODEL)),
        'norm_ffn': gain((DEPTH, D_MODEL)),
        'w_ada': nrm((DEPTH, D_MODEL, 6 * D_MODEL), 0.5 * D_MODEL ** -0.5),
        'b_ada': nrm((DEPTH, 6 * D_MODEL), 0.02),
        'w_in0': nrm((N_EVEN, D_MODEL, IN0_WIDTH), D_MODEL ** -0.5),
        'conv_w': nrm((N_EVEN, CONV_W, CONV_CH), CONV_W ** -0.5),
        'conv_b': nrm((N_EVEN, CONV_CH), 0.01),
        'dt_bias': dt0 + jnp.log(-jnp.expm1(-dt0)),
        'a_log': jnp.log(jax.random.uniform(next(ks), (N_EVEN, SSD_HEADS), f32, 1.0, 16.0)),
        'd_skip': gain((N_EVEN, SSD_HEADS)),
        'ssd_norm': gain((N_EVEN, SSD_WIDTH)),
        'q_a_norm': gain((N_EVEN, Q_LORA)),
        'w_uq': nrm((N_EVEN, Q_LORA, MLA_HEADS * (QK_NOPE + QK_ROPE)), Q_LORA ** -0.5),
        'q_norm_nope': gain((N_EVEN, QK_NOPE)),
        'q_norm_rope': gain((N_EVEN, QK_ROPE)),
        'kv_a_norm': gain((N_EVEN, KV_LORA)),
        'w_ukv': nrm((N_EVEN, KV_LORA, MLA_HEADS * (QK_NOPE + V_DIM)), KV_LORA ** -0.5),
        'k_norm_nope': gain((N_EVEN, QK_NOPE)),
        'k_norm_rope': gain((N_EVEN, QK_ROPE)),
        'w_out0': nrm((N_EVEN, MIX0_WIDTH, D_MODEL), MIX0_WIDTH ** -0.5),
        'w_in1': nrm((N_ODD, D_MODEL, IN1_WIDTH), D_MODEL ** -0.5),
        'w_out1': nrm((N_ODD, RET_V, D_MODEL), RET_V ** -0.5),
        'w_gate': nrm((DEPTH, D_MODEL, D_FF), D_MODEL ** -0.5),
        'w_up': nrm((DEPTH, D_MODEL, D_FF), D_MODEL ** -0.5),
        'w_down': nrm((DEPTH, D_FF, D_MODEL), D_FF ** -0.5),
    }


def reference(x_prompt, x_sample, c_prompt, c_sample, cache_mla_ckv, cache_mla_kpe, state_ssd, state_ssd_conv,
              state_ret, norm_mix, norm_ffn, w_ada, b_ada, w_in0, conv_w, conv_b, dt_bias, a_log, d_skip,
              ssd_norm, q_a_norm, w_uq, q_norm_nope, q_norm_rope, kv_a_norm, w_ukv, k_norm_nope, k_norm_rope,
              w_out0, w_in1, w_out1, w_gate, w_up, w_down):
    P = {'norm_mix': norm_mix, 'norm_ffn': norm_ffn, 'w_ada': w_ada, 'b_ada': b_ada, 'w_in0': w_in0,
         'conv_w': conv_w, 'conv_b': conv_b, 'dt_bias': dt_bias, 'a_log': a_log, 'd_skip': d_skip,
         'ssd_norm': ssd_norm, 'q_a_norm': q_a_norm, 'w_uq': w_uq, 'q_norm_nope': q_norm_nope,
         'q_norm_rope': q_norm_rope, 'kv_a_norm': kv_a_norm, 'w_ukv': w_ukv, 'k_norm_nope': k_norm_nope,
         'k_norm_rope': k_norm_rope, 'w_out0': w_out0, 'w_in1': w_in1, 'w_out1': w_out1,
         'w_gate': w_gate, 'w_up': w_up, 'w_down': w_down}
    caches = {'ckv': cache_mla_ckv, 'kpe': cache_mla_kpe, 'ssd': state_ssd, 'conv': state_ssd_conv, 'ret': state_ret}
    pos_p = jnp.arange(x_prompt.shape[1])
    pos_s = PAST_LEN + jnp.arange(x_sample.shape[1])
    y_prompt, p_ckv, p_kpe, p_ssd, p_conv, p_ret = run_trunk(x_prompt, c_prompt, pos_p, None, P)
    y_sample, s_ckv, s_kpe, s_ssd, s_conv, s_ret = run_trunk(x_sample, c_sample, pos_s, caches, P)
    return (y_prompt, y_sample, p_ckv, p_kpe, p_ssd, p_conv, p_ret, s_ckv, s_kpe, s_ssd, s_conv, s_ret)
```

```python
import functools
import math

import jax
import jax.numpy as jnp
from jax import lax
from jax.experimental import pallas as pl
from jax.experimental.pallas import tpu as pltpu

F32 = jnp.float32
BF16 = jnp.bfloat16

CHUNK = 64
EPS = 1e-6
NEG_BIG = -1e30

SSD_HEADDIM = 64
SSD_GROUPS = 8
SSD_STATE = 128
CONV_W = 4
MLA_HEADS = 16
QK_NOPE = 128
QK_ROPE = 64
V_DIM = 128
Q_LORA = 1024
KV_LORA = 512
ROPE_THETA = 10000.0
MLA_SCALE = (QK_NOPE + QK_ROPE) ** -0.5
RET_HEADS = 16
LANE = 128
FF_ALIGN = 1024

VMEM_LIMIT = 56 * 1024 * 1024


def _cparams(sem):
    return pltpu.CompilerParams(dimension_semantics=sem, vmem_limit_bytes=VMEM_LIMIT)


def _sigmoid(x):
    return 1.0 / (1.0 + jnp.exp(-x))


def _silu(x):
    return x * _sigmoid(x)


def _softplus(x):
    return jnp.maximum(x, 0.0) + jnp.log1p(jnp.exp(-jnp.abs(x)))


def _dot(a, b):
    return jnp.dot(a, b, preferred_element_type=F32)


def _dot_nt(a, b):
    return lax.dot_general(a, b, (((1,), (1,)), ((), ())), preferred_element_type=F32)


def _dot_tn(a, b):
    return lax.dot_general(a, b, (((0,), (0,)), ((), ())), preferred_element_type=F32)


def _ada_kernel(c_ref, w_ref, b_ref, o_ref):
    c = c_ref[...]
    o_ref[...] = _dot(_silu(c).astype(BF16), w_ref[...].astype(BF16)) + b_ref[...]


def _ada(c_all, w_ada, b_ada):
    depth, d, n = w_ada.shape
    nb = c_all.shape[0]
    tn = 512
    return pl.pallas_call(
        _ada_kernel,
        out_shape=jax.ShapeDtypeStruct((depth, nb, n), F32),
        grid=(depth, n // tn),
        in_specs=[
            pl.BlockSpec((nb, d), lambda l, j: (0, 0)),
            pl.BlockSpec((None, d, tn), lambda l, j: (l, 0, j)),
            pl.BlockSpec((None, 1, tn), lambda l, j: (l, 0, j)),
        ],
        out_specs=pl.BlockSpec((None, nb, tn), lambda l, j: (l, 0, j)),
        compiler_params=_cparams(("arbitrary", "arbitrary")),
        name="ada",
    )(c_all, w_ada, b_ada.reshape(depth, 1, n))


def _normmod_kernel(x_ref, g_ref, sc_ref, sh_ref, o_ref):
    x = x_ref[...]
    rb, d = x.shape
    y = x * lax.rsqrt(jnp.mean(x * x, axis=-1, keepdims=True) + EPS) * g_ref[...]
    y = y.reshape(rb // CHUNK, CHUNK, d)
    y = y * (1.0 + sc_ref[...][:, None, :]) + sh_ref[...][:, None, :]
    o_ref[...] = y.reshape(rb, d).astype(o_ref.dtype)


def _normmod(x, gain, sc_tbl, sh_tbl):
    r, d = x.shape
    rb = 512
    gb = rb // CHUNK
    return pl.pallas_call(
        _normmod_kernel,
        out_shape=jax.ShapeDtypeStruct((r, d), BF16),
        grid=(r // rb,),
        in_specs=[
            pl.BlockSpec((rb, d), lambda i: (i, 0)),
            pl.BlockSpec((1, d), lambda i: (0, 0)),
            pl.BlockSpec((gb, d), lambda i: (i, 0)),
            pl.BlockSpec((gb, d), lambda i: (i, 0)),
        ],
        out_specs=pl.BlockSpec((rb, d), lambda i: (i, 0)),
        compiler_params=_cparams(("arbitrary",)),
        name="normmod",
    )(x, gain.reshape(1, d), sc_tbl, sh_tbl)


def _mm_kernel(*refs, n_w, n_ex, nk, epi):
    x_ref = refs[0]
    w_refs = refs[1:1 + n_w]
    ex_refs = refs[1 + n_w:1 + n_w + n_ex]
    o_ref = refs[1 + n_w + n_ex]
    acc_refs = refs[2 + n_w + n_ex:]
    j = pl.program_id(1)
    if nk == 1:
        accs = [_dot(x_ref[...], w[...]) for w in w_refs]
        o_ref[...] = epi(accs, ex_refs, j).astype(o_ref.dtype)
    else:
        k = pl.program_id(2)

        @pl.when(k == 0)
        def _():
            for a in acc_refs:
                a[...] = jnp.zeros_like(a)

        for a, w in zip(acc_refs, w_refs):
            a[...] += _dot(x_ref[...], w[...])

        @pl.when(k == nk - 1)
        def _():
            o_ref[...] = epi([a[...] for a in acc_refs], ex_refs, j).astype(o_ref.dtype)


def _mm(x, ws, *, bm, bn, bk=None, out_dtype, epi, extras=(), name):
    m, kdim = x.shape
    n = ws[0].shape[1]
    bk = kdim if bk is None else bk
    nk = kdim // bk
    in_specs = [pl.BlockSpec((bm, bk), lambda i, j, k: (i, k))]
    in_specs += [pl.BlockSpec((bk, bn), lambda i, j, k: (k, j)) for _ in ws]
    for _, bs, im in extras:
        in_specs.append(pl.BlockSpec(bs, functools.partial(lambda i, j, k, im: im(i, j), im=im)))
    scratch = [pltpu.VMEM((bm, bn), F32) for _ in ws] if nk > 1 else []
    return pl.pallas_call(
        functools.partial(_mm_kernel, n_w=len(ws), n_ex=len(extras), nk=nk, epi=epi),
        out_shape=jax.ShapeDtypeStruct((m, n), out_dtype),
        grid=(m // bm, n // bn, nk),
        in_specs=in_specs,
        out_specs=pl.BlockSpec((bm, bn), lambda i, j, k: (i, j)),
        scratch_shapes=scratch,
        compiler_params=_cparams(("arbitrary", "arbitrary", "arbitrary")),
        name=name,
    )(x, *ws, *[e[0] for e in extras])


def _epi_plain(accs, ex, j):
    return accs[0]


def _epi_swiglu(accs, ex, j):
    return _silu(accs[0]) * accs[1]


def _epi_residual(accs, ex, j):
    res_ref, gate_ref = ex
    acc = accs[0]
    bm, bn = acc.shape
    upd = acc.reshape(bm // CHUNK, CHUNK, bn) * gate_ref[...][:, None, :]
    return res_ref[...] + upd.reshape(bm, bn)


def _epi_rope_qk(accs, ex, j, *, head_dim, k_block0, k_scale):
    cos_ref, sin_ref = ex
    acc = accs[0]
    c, s = cos_ref[...], sin_ref[...]
    half = head_dim // 2
    scale = jnp.where(j >= k_block0, k_scale, 1.0).astype(F32)
    outs = []
    for h in range(acc.shape[1] // head_dim):
        x1 = acc[:, h * head_dim:h * head_dim + half]
        x2 = acc[:, h * head_dim + half:(h + 1) * head_dim]
        outs.append((x1 * c - x2 * s) * scale)
        outs.append((x1 * s + x2 * c) * scale)
    return jnp.concatenate(outs, axis=-1)


def _residual_mm(x, w, res, gate_tbl, *, bm, bn, bk=None, name):
    gb = bm // CHUNK
    return _mm(x, [w], bm=bm, bn=bn, bk=bk, out_dtype=F32, epi=_epi_residual,
               extras=[(res, (bm, bn), lambda i, j: (i, j)), (gate_tbl, (gb, bn), lambda i, j: (i, j))],
               name=name)


def _seq_flags(i, npg, ncp):
    is_p = i < npg
    first = jnp.logical_or(jnp.logical_not(is_p), (i % ncp) == 0)
    return is_p, first


def _seq_index(i, npg, ncp, bp):
    return jnp.where(i < npg, i // ncp, bp + i - npg)


def _ssd_kernel(z_ref, xa_ref, xb_ref, sm_ref, cinit_ref, h0_ref, cw_ref, cb_ref, dtb_ref, alog_ref, dsk_ref,
                nrm_ref, y_ref, hout_ref, cout_ref, ext, xc, ysc, *, npg, ncp):
    i = pl.program_id(0)
    is_p, first = _seq_flags(i, npg, ncp)
    width = xa_ref.shape[1]
    heads = width // SSD_HEADDIM
    rpg = heads // SSD_GROUPS
    gw = rpg * SSD_HEADDIM
    L = CHUNK

    @pl.when(jnp.logical_and(first, is_p))
    def _():
        ext[0:8, :] = jnp.zeros((8, ext.shape[1]), F32)
        hout_ref[...] = jnp.zeros_like(hout_ref)

    @pl.when(jnp.logical_and(first, jnp.logical_not(is_p)))
    def _():
        ext[5:8, :] = cinit_ref[...]
        hout_ref[...] = h0_ref[...]

    ext[8:8 + L, 0:width] = xa_ref[...]
    ext[8:8 + L, width:2 * width] = xb_ref[...]

    cstep = 512
    for c in range(2 * width // cstep):
        sl = slice(c * cstep, (c + 1) * cstep)
        a = cb_ref[:, sl] + cw_ref[0:1, sl] * ext[5:5 + L, sl]
        for w in range(1, CONV_W):
            a = a + cw_ref[w:w + 1, sl] * ext[5 + w:5 + w + L, sl]
        xc[:, sl] = _silu(a)
    tail_rows = ext[L:L + 8, :]
    cout_ref[...] = tail_rows
    ext[0:8, :] = tail_rows

    dt = _softplus(sm_ref[:, 0:LANE] + dtb_ref[...])
    a_neg = -jnp.exp(alog_ref[...])
    cs = dt * a_neg
    row = lax.broadcasted_iota(jnp.int32, (L, LANE), 0)
    sh = 1
    while sh < L:
        cs = cs + jnp.where(row >= sh, pltpu.roll(cs, sh, axis=0), 0.0)
        sh *= 2
    both_t = jnp.concatenate([cs, dt], axis=0).T
    cs_t = both_t[:, 0:L]
    dt_t = both_t[:, L:2 * L]
    tail_t = jnp.exp(cs_t[:, L - 1:L] - cs_t) * dt_t
    ecs = jnp.exp(cs)

    ii = lax.broadcasted_iota(jnp.int32, (L, L), 0)
    jj = lax.broadcasted_iota(jnp.int32, (L, L), 1)
    causal = ii >= jj

    half = width // 2
    x_t = jnp.concatenate([xc[:, 0:half], xc[:, half:width]], axis=0).T

    for g in range(SSD_GROUPS):
        bg = xc[:, width + g * SSD_STATE:width + (g + 1) * SSD_STATE].astype(BF16)
        cg = xc[:, width + SSD_GROUPS * SSD_STATE + g * SSD_STATE:
                width + SSD_GROUPS * SSD_STATE + (g + 1) * SSD_STATE].astype(BF16)
        cbm = _dot_nt(cg, bg)
        hg = hout_ref[g * gw:(g + 1) * gw, :]
        ystate = _dot_nt(cg, hg.astype(BF16))
        lo = (g * gw) % half
        lanes = slice(0, L) if g * gw < half else slice(L, 2 * L)
        xs_rows, dec_rows = [], []
        for r in range(rpg):
            h = g * rpg + r
            cs_col = cs[:, h:h + 1]
            seg = cs_col - cs_t[h:h + 1, :]
            dec = jnp.exp(jnp.where(causal, seg, NEG_BIG))
            wts = cbm * dec * dt_t[h:h + 1, :]
            xh = xc[:, h * SSD_HEADDIM:(h + 1) * SSD_HEADDIM]
            yh = _dot(wts.astype(BF16), xh.astype(BF16))
            yh = yh + ystate[:, r * SSD_HEADDIM:(r + 1) * SSD_HEADDIM] * ecs[:, h:h + 1]
            yh = yh + xh * dsk_ref[:, h:h + 1]
            ysc[:, h * SSD_HEADDIM:(h + 1) * SSD_HEADDIM] = yh
            xs_rows.append(x_t[lo + r * SSD_HEADDIM:lo + (r + 1) * SSD_HEADDIM, lanes] * tail_t[h:h + 1, :])
            dec_rows.append(jnp.broadcast_to(jnp.exp(cs_t[h:h + 1, L - 1:L]), (SSD_HEADDIM, SSD_STATE)))
        upd = _dot(jnp.concatenate(xs_rows, axis=0).astype(BF16), bg)
        hout_ref[g * gw:(g + 1) * gw, :] = hg * jnp.concatenate(dec_rows, axis=0) + upd

    y = ysc[...] * _silu(z_ref[...])
    outs = []
    for g in range(SSD_GROUPS):
        yg = y[:, g * gw:(g + 1) * gw]
        ms = jnp.mean(yg * yg, axis=-1, keepdims=True)
        outs.append(yg * lax.rsqrt(ms + EPS) * nrm_ref[:, g * gw:(g + 1) * gw])
    y_ref[...] = jnp.concatenate(outs, axis=-1).astype(y_ref.dtype)


def _ssd(proj, small, conv_state, ssd_state, conv_w, conv_b, dtb, alog, dsk, nrm, *, bp, ncp, bs, mix_width):
    r = proj.shape[0]
    ng = r // CHUNK
    npg = bp * ncp
    width = nrm.shape[1]
    nseq = bp + bs
    seq = lambda i: _seq_index(i, npg, ncp, bp)
    samp = lambda i: jnp.maximum(i - npg, 0)
    return pl.pallas_call(
        functools.partial(_ssd_kernel, npg=npg, ncp=ncp),
        out_shape=(jax.ShapeDtypeStruct((r, mix_width), BF16),
                   jax.ShapeDtypeStruct((nseq, width, SSD_STATE), F32),
                   jax.ShapeDtypeStruct((nseq, 8, 2 * width), F32)),
        grid=(ng,),
        in_specs=[
            pl.BlockSpec((CHUNK, width), lambda i: (i, 0)),
            pl.BlockSpec((CHUNK, width), lambda i: (i, 1)),
            pl.BlockSpec((CHUNK, width), lambda i: (i, 2)),
            pl.BlockSpec((CHUNK, 2 * LANE), lambda i: (i, 0)),
            pl.BlockSpec((None, CONV_W - 1, 2 * width), lambda i: (samp(i), 0, 0)),
            pl.BlockSpec((None, width, SSD_STATE), lambda i: (samp(i), 0, 0)),
            pl.BlockSpec((CONV_W, 2 * width), lambda i: (0, 0)),
            pl.BlockSpec((1, 2 * width), lambda i: (0, 0)),
            pl.BlockSpec((1, LANE), lambda i: (0, 0)),
            pl.BlockSpec((1, LANE), lambda i: (0, 0)),
            pl.BlockSpec((1, LANE), lambda i: (0, 0)),
            pl.BlockSpec((1, width), lambda i: (0, 0)),
        ],
        out_specs=(
            pl.BlockSpec((CHUNK, width), lambda i: (i, 0)),
            pl.BlockSpec((None, width, SSD_STATE), lambda i: (seq(i), 0, 0)),
            pl.BlockSpec((None, 8, 2 * width), lambda i: (seq(i), 0, 0)),
        ),
        scratch_shapes=[pltpu.VMEM((CHUNK + 8, 2 * width), F32), pltpu.VMEM((CHUNK, 2 * width), F32),
                        pltpu.VMEM((CHUNK, width), F32)],
        compiler_params=_cparams(("arbitrary",)),
        name="ssd",
    )(proj, proj, proj, small, conv_state, ssd_state, conv_w, conv_b, dtb, alog, dsk, nrm)


def _rope_pad(x, cos_t, sin_t):
    return x * cos_t + pltpu.roll(x, LANE // 2, axis=1) * sin_t


def _mla_q_kernel(cq_ref, ga_ref, w_ref, gn_ref, gr_ref, cos_ref, sin_ref, q_ref):
    cq = cq_ref[...]
    xn = cq * lax.rsqrt(jnp.mean(cq * cq, axis=-1, keepdims=True) + EPS) * ga_ref[...]
    q = _dot(xn.astype(BF16), w_ref[...])
    c, s = cos_ref[...], sin_ref[...]
    nope_w = MLA_HEADS * QK_NOPE
    for h in range(MLA_HEADS):
        qn = q[:, h * QK_NOPE:(h + 1) * QK_NOPE]
        qn = qn * lax.rsqrt(jnp.mean(qn * qn, axis=-1, keepdims=True) + EPS) * gn_ref[...]
        qp = q[:, nope_w + h * LANE:nope_w + (h + 1) * LANE]
        qp = qp * lax.rsqrt(jnp.sum(qp * qp, axis=-1, keepdims=True) * (1.0 / QK_ROPE) + EPS) * gr_ref[...]
        qp = _rope_pad(qp, c, s)
        q_ref[h] = (jnp.concatenate([qn, qp], axis=-1) * MLA_SCALE).astype(q_ref.dtype)


def _mla_q(proj, cq_block, ga, w_uq_p, gn, gr_pad, cos_t, sin_t):
    r = proj.shape[0]
    bm = 512
    n = w_uq_p.shape[1]
    return pl.pallas_call(
        _mla_q_kernel,
        out_shape=jax.ShapeDtypeStruct((MLA_HEADS, r, 2 * LANE), BF16),
        grid=(r // bm,),
        in_specs=[
            pl.BlockSpec((bm, Q_LORA), lambda i: (i, cq_block)),
            pl.BlockSpec((1, Q_LORA), lambda i: (0, 0)),
            pl.BlockSpec((Q_LORA, n), lambda i: (0, 0)),
            pl.BlockSpec((1, QK_NOPE), lambda i: (0, 0)),
            pl.BlockSpec((1, LANE), lambda i: (0, 0)),
            pl.BlockSpec((bm, LANE), lambda i: (i, 0)),
            pl.BlockSpec((bm, LANE), lambda i: (i, 0)),
        ],
        out_specs=pl.BlockSpec((MLA_HEADS, bm, 2 * LANE), lambda i: (0, i, 0)),
        compiler_params=_cparams(("arbitrary",)),
        name="mla_q",
    )(proj, ga, w_uq_p, gn, gr_pad, cos_t, sin_t)


def _mla_ckv_kernel(ckv_ref, sm_ref, gkv_ref, gr_ref, cos_ref, sin_ref, ckv_out, kpe_out):
    x = ckv_ref[...]
    ckv_out[...] = x * lax.rsqrt(jnp.mean(x * x, axis=-1, keepdims=True) + EPS) * gkv_ref[...]
    kp = sm_ref[:, LANE:2 * LANE]
    kp = kp * lax.rsqrt(jnp.sum(kp * kp, axis=-1, keepdims=True) * (1.0 / QK_ROPE) + EPS) * gr_ref[...]
    kpe_out[...] = _rope_pad(kp, cos_ref[...], sin_ref[...])


def _mla_ckv(proj, small, ckv_block, gkv, gr_pad, cos_t, sin_t):
    r = proj.shape[0]
    bm = 512
    return pl.pallas_call(
        _mla_ckv_kernel,
        out_shape=(jax.ShapeDtypeStruct((r, KV_LORA), F32), jax.ShapeDtypeStruct((r, LANE), F32)),
        grid=(r // bm,),
        in_specs=[
            pl.BlockSpec((bm, KV_LORA), lambda i: (i, ckv_block)),
            pl.BlockSpec((bm, 2 * LANE), lambda i: (i, 0)),
            pl.BlockSpec((1, KV_LORA), lambda i: (0, 0)),
            pl.BlockSpec((1, LANE), lambda i: (0, 0)),
            pl.BlockSpec((bm, LANE), lambda i: (i, 0)),
            pl.BlockSpec((bm, LANE), lambda i: (i, 0)),
        ],
        out_specs=(pl.BlockSpec((bm, KV_LORA), lambda i: (i, 0)), pl.BlockSpec((bm, LANE), lambda i: (i, 0))),
        compiler_params=_cparams(("arbitrary",)),
        name="mla_ckv",
    )(proj, small, gkv, gr_pad, cos_t, sin_t)


def _mla_kvup_kernel(ckv_ref, kpe_ref, w_ref, gk_ref, k_ref, v_ref):
    kv = _dot(ckv_ref[...].astype(BF16), w_ref[...])
    kpe = kpe_ref[...]
    nope_w = MLA_HEADS * QK_NOPE
    for h in range(MLA_HEADS):
        kn = kv[:, h * QK_NOPE:(h + 1) * QK_NOPE]
        kn = kn * lax.rsqrt(jnp.mean(kn * kn, axis=-1, keepdims=True) + EPS) * gk_ref[...]
        k_ref[h] = jnp.concatenate([kn, kpe], axis=-1).astype(k_ref.dtype)
        v_ref[h] = kv[:, nope_w + h * V_DIM:nope_w + (h + 1) * V_DIM].astype(v_ref.dtype)


def _mla_kvup(ckv, kpe_pad, w_ukv_p, gk):
    r = ckv.shape[0]
    bm = 512
    n = w_ukv_p.shape[1]
    return pl.pallas_call(
        _mla_kvup_kernel,
        out_shape=(jax.ShapeDtypeStruct((MLA_HEADS, r, 2 * LANE), BF16),
                   jax.ShapeDtypeStruct((MLA_HEADS, r, V_DIM), BF16)),
        grid=(r // bm,),
        in_specs=[
            pl.BlockSpec((bm, KV_LORA), lambda i: (i, 0)),
            pl.BlockSpec((bm, LANE), lambda i: (i, 0)),
            pl.BlockSpec((KV_LORA, n), lambda i: (0, 0)),
            pl.BlockSpec((1, QK_NOPE), lambda i: (0, 0)),
        ],
        out_specs=(pl.BlockSpec((MLA_HEADS, bm, 2 * LANE), lambda i: (0, i, 0)),
                   pl.BlockSpec((MLA_HEADS, bm, V_DIM), lambda i: (0, i, 0))),
        compiler_params=_cparams(("arbitrary",)),
        name="mla_kvup",
    )(ckv, kpe_pad, w_ukv_p, gk)


def _attn_prompt_kernel(q_ref, k_ref, v_ref, mix_ref, o_ref, *, tq):
    del mix_ref
    qi = pl.program_id(2)
    q = q_ref[...]
    row = qi * tq + lax.broadcasted_iota(jnp.int32, (tq, tq), 0)
    limit = (row // CHUNK + 1) * CHUNK
    col0 = lax.broadcasted_iota(jnp.int32, (tq, tq), 1)

    def body(kb, carry):
        m, l, acc = carry
        start = pl.multiple_of(kb * tq, tq)
        k = k_ref[pl.ds(start, tq), :]
        v = v_ref[pl.ds(start, tq), :]
        s = _dot_nt(q, k)
        s = jnp.where(col0 + kb * tq < limit, s, NEG_BIG)
        m_new = jnp.maximum(m, jnp.max(s, axis=-1, keepdims=True))
        p = jnp.exp(s - m_new)
        alpha = jnp.exp(m - m_new)
        l = alpha * l + jnp.sum(p, axis=-1, keepdims=True)
        acc = alpha * acc + _dot(p.astype(BF16), v)
        return m_new, l, acc

    m0 = jnp.full((tq, 1), NEG_BIG, F32)
    l0 = jnp.zeros((tq, 1), F32)
    a0 = jnp.zeros((tq, V_DIM), F32)
    m, l, acc = lax.fori_loop(0, qi + 1, body, (m0, l0, a0))
    o_ref[...] = (acc / l).astype(o_ref.dtype)


def _attn_prompt(q, k, v, mix, *, bp, tp):
    tq = 256
    nq = tp // tq
    r, mw = mix.shape
    col0 = (mw // 2) // V_DIM
    return pl.pallas_call(
        functools.partial(_attn_prompt_kernel, tq=tq),
        out_shape=jax.ShapeDtypeStruct((r, mw), mix.dtype),
        grid=(bp, MLA_HEADS, nq),
        in_specs=[
            pl.BlockSpec((None, tq, 2 * LANE), lambda b, h, i: (h, b * nq + i, 0)),
            pl.BlockSpec((None, tp, 2 * LANE), lambda b, h, i: (h, b, 0)),
            pl.BlockSpec((None, tp, V_DIM), lambda b, h, i: (h, b, 0)),
            pl.BlockSpec(memory_space=pl.ANY),
        ],
        out_specs=pl.BlockSpec((tq, V_DIM), lambda b, h, i: (b * nq + i, col0 + h)),
        input_output_aliases={3: 0},
        compiler_params=_cparams(("arbitrary", "arbitrary", "arbitrary")),
        name="attn_prompt",
    )(q, k, v, mix)


def _attn_sample_kernel(q_ref, kp_ref, vp_ref, kn_ref, vn_ref, mix_ref, o_ref, *, past):
    del mix_ref
    q = q_ref[...]
    ts = q.shape[0]
    sp = _dot_nt(q, kp_ref[...])
    sn = _dot_nt(q, kn_ref[...])
    row = past + lax.broadcasted_iota(jnp.int32, (ts, ts), 0)
    col = past + lax.broadcasted_iota(jnp.int32, (ts, ts), 1)
    sn = jnp.where(col < (row // CHUNK + 1) * CHUNK, sn, NEG_BIG)
    m = jnp.maximum(jnp.max(sp, axis=-1, keepdims=True), jnp.max(sn, axis=-1, keepdims=True))
    pp = jnp.exp(sp - m)
    pn = jnp.exp(sn - m)
    l = jnp.sum(pp, axis=-1, keepdims=True) + jnp.sum(pn, axis=-1, keepdims=True)
    o = _dot(pp.astype(BF16), vp_ref[...]) + _dot(pn.astype(BF16), vn_ref[...])
    o_ref[...] = (o / l).astype(o_ref.dtype)


def _attn_sample(q, k_past, v_past, k_new, v_new, mix, *, bs, ts, past, row0):
    r, mw = mix.shape
    col0 = (mw // 2) // V_DIM
    g0 = row0 // ts
    return pl.pallas_call(
        functools.partial(_attn_sample_kernel, past=past),
        out_shape=jax.ShapeDtypeStruct((r, mw), mix.dtype),
        grid=(bs, MLA_HEADS),
        in_specs=[
            pl.BlockSpec((None, ts, 2 * LANE), lambda b, h: (h, g0 + b, 0)),
            pl.BlockSpec((None, past, 2 * LANE), lambda b, h: (h, b, 0)),
            pl.BlockSpec((None, past, V_DIM), lambda b, h: (h, b, 0)),
            pl.BlockSpec((None, ts, 2 * LANE), lambda b, h: (h, g0 + b, 0)),
            pl.BlockSpec((None, ts, V_DIM), lambda b, h: (h, g0 + b, 0)),
            pl.BlockSpec(memory_space=pl.ANY),
        ],
        out_specs=pl.BlockSpec((ts, V_DIM), lambda b, h: (g0 + b, col0 + h)),
        input_output_aliases={5: 0},
        compiler_params=_cparams(("arbitrary", "arbitrary")),
        name="attn_sample",
    )(q, k_past, v_past, k_new, v_new, mix)


def _ret_kernel(qk_ref, v_ref, g_ref, s0_ref, dm_ref, qd_ref, kd_ref, cd_ref, o_ref, s_ref, *, npg, ncp):
    i = pl.program_id(0)
    is_p, first = _seq_flags(i, npg, ncp)
    heads, kdim, vdim = s_ref.shape

    @pl.when(jnp.logical_and(first, is_p))
    def _():
        s_ref[...] = jnp.zeros_like(s_ref)

    @pl.when(jnp.logical_and(first, jnp.logical_not(is_p)))
    def _():
        s_ref[...] = s0_ref[...]

    for h in range(heads):
        q = qk_ref[:, h * kdim:(h + 1) * kdim]
        k = qk_ref[:, (heads + h) * kdim:(heads + h + 1) * kdim]
        v = v_ref[:, h * vdim:(h + 1) * vdim]
        s = s_ref[h]
        att = _dot_nt(q, k) * dm_ref[h]
        o = _dot(att.astype(BF16), v) + _dot(q, s.astype(BF16)) * qd_ref[:, h:h + 1]
        kdec = (k.astype(F32) * kd_ref[:, h:h + 1]).astype(BF16)
        s_ref[h] = s * cd_ref[:, h:h + 1] + _dot_tn(kdec, v)
        mu = jnp.mean(o, axis=-1, keepdims=True)
        oc = o - mu
        var = jnp.mean(oc * oc, axis=-1, keepdims=True)
        g = g_ref[:, h * vdim:(h + 1) * vdim].astype(F32)
        o_ref[:, h * vdim:(h + 1) * vdim] = (_silu(g) * (oc * lax.rsqrt(var + EPS))).astype(o_ref.dtype)


def _retention(qk, vg, s0, dmask, qdec, kdec, cdec, *, bp, ncp, bs):
    r = qk.shape[0]
    ng = r // CHUNK
    npg = bp * ncp
    heads, kdim, vdim = s0.shape[1:]
    seq = lambda i: _seq_index(i, npg, ncp, bp)
    samp = lambda i: jnp.maximum(i - npg, 0)
    return pl.pallas_call(
        functools.partial(_ret_kernel, npg=npg, ncp=ncp),
        out_shape=(jax.ShapeDtypeStruct((r, heads * vdim), BF16),
                   jax.ShapeDtypeStruct((bp + bs, heads, kdim, vdim), F32)),
        grid=(ng,),
        in_specs=[
            pl.BlockSpec((CHUNK, 2 * heads * kdim), lambda i: (i, 0)),
            pl.BlockSpec((CHUNK, heads * vdim), lambda i: (i, 0)),
            pl.BlockSpec((CHUNK, heads * vdim), lambda i: (i, 1)),
            pl.BlockSpec((None, heads, kdim, vdim), lambda i: (samp(i), 0, 0, 0)),
            pl.BlockSpec((heads, CHUNK, CHUNK), lambda i: (0, 0, 0)),
            pl.BlockSpec((CHUNK, heads), lambda i: (0, 0)),
            pl.BlockSpec((CHUNK, heads), lambda i: (0, 0)),
            pl.BlockSpec((1, heads), lambda i: (0, 0)),
        ],
        out_specs=(pl.BlockSpec((CHUNK, heads * vdim), lambda i: (i, 0)),
                   pl.BlockSpec((None, heads, kdim, vdim), lambda i: (seq(i), 0, 0, 0))),
        compiler_params=_cparams(("arbitrary",)),
        name="retention",
    )(qk, vg, vg, s0, dmask, qdec, kdec, cdec)


def _pad_rope_cols(a):
    half = QK_ROPE // 2
    z = jnp.zeros(a.shape[:-1] + (LANE // 2 - half,), a.dtype)
    return jnp.concatenate([a[..., :half], z, a[..., half:], z], axis=-1)


def _unpad_rope_cols(a):
    half = QK_ROPE // 2
    return jnp.concatenate([a[..., :half], a[..., LANE // 2:LANE // 2 + half]], axis=-1)


def _rope_tables(pos, half):
    inv = ROPE_THETA ** (-jnp.arange(half, dtype=F32) / half)
    ang = pos.astype(F32)[:, None] * inv[None, :]
    return jnp.cos(ang), jnp.sin(ang)


def _retention_tables(heads):
    L = CHUNK
    lg = jnp.log1p(-jnp.exp2(-5.0 - jnp.arange(heads, dtype=F32)))
    idx = jnp.arange(L, dtype=F32)
    rel = idx[:, None] - idx[None, :]
    dmask = jnp.exp(jnp.where(rel[None] >= 0, rel[None] * lg[:, None, None], -jnp.inf))
    qdec = jnp.exp((idx[:, None] + 1.0) * lg[None, :])
    kdec = jnp.exp((L - 1.0 - idx[:, None]) * lg[None, :])
    cdec = jnp.exp(L * lg)[None, :]
    return dmask, qdec, kdec, cdec


@jax.jit
def kernel(x_prompt, x_sample, c_prompt, c_sample, cache_mla_ckv, cache_mla_kpe, state_ssd, state_ssd_conv, state_ret, norm_mix, norm_ffn, w_ada, b_ada, w_in0, conv_w, conv_b, dt_bias, a_log, d_skip, ssd_norm, q_a_norm, w_uq, q_norm_nope, q_norm_rope, kv_a_norm, w_ukv, k_norm_nope, k_norm_rope, w_out0, w_in1, w_out1, w_gate, w_up, w_down):
    bp, tp, d = x_prompt.shape
    bs, ts, _ = x_sample.shape
    past = cache_mla_ckv.shape[2]
    assert ts == CHUNK and tp % CHUNK == 0 and past % CHUNK == 0
    rp, rs = bp * tp, bs * ts
    r = rp + rs
    ncp = tp // CHUNK
    bm = min(1024, r)
    assert r % bm == 0 and bm % 512 == 0

    ssd_width = d // 2
    ssd_heads = ssd_width // SSD_HEADDIM
    conv_ch = ssd_width + 2 * SSD_GROUPS * SSD_STATE
    off_dt = ssd_width + conv_ch
    off_cq = off_dt + ssd_heads
    off_ckv = off_cq + Q_LORA
    off_kpe = off_ckv + KV_LORA
    assert conv_ch == 2 * ssd_width and ssd_heads <= LANE
    mix0 = ssd_width + MLA_HEADS * V_DIM
    ret_kdim = d // RET_HEADS
    ret_qk = RET_HEADS * ret_kdim
    d_ff = w_gate.shape[2]
    d_ffp = -(-d_ff // FF_ALIGN) * FF_ALIGN

    wi = w_in0[0]
    w0_main = jnp.concatenate([wi[:, :off_dt], wi[:, off_cq:off_kpe]], axis=1).astype(BF16)
    w0_small = jnp.concatenate([wi[:, off_dt:off_cq], jnp.zeros((d, LANE - ssd_heads), F32),
                                _pad_rope_cols(wi[:, off_kpe:])], axis=1).astype(BF16)
    wq = w_uq[0].reshape(Q_LORA, MLA_HEADS, QK_NOPE + QK_ROPE)
    w_uq_p = jnp.concatenate([wq[:, :, :QK_NOPE].reshape(Q_LORA, -1),
                              _pad_rope_cols(wq[:, :, QK_NOPE:]).reshape(Q_LORA, -1)], axis=1).astype(BF16)
    wkv = w_ukv[0].reshape(KV_LORA, MLA_HEADS, QK_NOPE + V_DIM)
    w_ukv_p = jnp.concatenate([wkv[:, :, :QK_NOPE].reshape(KV_LORA, -1),
                               wkv[:, :, QK_NOPE:].reshape(KV_LORA, -1)], axis=1).astype(BF16)
    w_out0_b = w_out0[0].astype(BF16)
    w1_qk = w_in1[0][:, :2 * ret_qk].astype(BF16)
    w1_vg = w_in1[0][:, 2 * ret_qk:].astype(BF16)
    w_out1_b = w_out1[0].astype(BF16)
    ffpad = d_ffp - d_ff
    w_gate_b = jnp.pad(w_gate, ((0, 0), (0, 0), (0, ffpad))).astype(BF16)
    w_up_b = jnp.pad(w_up, ((0, 0), (0, 0), (0, ffpad))).astype(BF16)
    w_down_b = jnp.pad(w_down, ((0, 0), (0, ffpad), (0, 0))).astype(BF16)

    def lane_pad(a, n=LANE):
        return jnp.pad(a, (0, n - a.shape[0])).reshape(1, n)

    pos = jnp.concatenate([jnp.tile(jnp.arange(tp), bp), jnp.tile(past + jnp.arange(ts), bs)])
    c32, s32 = _rope_tables(pos, QK_ROPE // 2)
    zq = jnp.zeros_like(c32)
    cos_m = jnp.concatenate([c32, zq, c32, zq], axis=1)
    sin_m = jnp.concatenate([-s32, zq, s32, zq], axis=1)
    cos_r, sin_r = _rope_tables(pos, ret_kdim // 2)
    dmask, qdec, kdec, cdec = _retention_tables(RET_HEADS)

    nb = -(-(bp + bs) // 8) * 8
    c_all = jnp.concatenate([c_prompt, c_sample, jnp.zeros((nb - bp - bs, d), F32)], axis=0)
    mod = _ada(c_all, w_ada, b_ada)
    depth = mod.shape[0]
    mod_p = jnp.broadcast_to(mod[:, :bp, None, :], (depth, bp, ncp, 6 * d)).reshape(depth, bp * ncp, 6 * d)
    tbl = jnp.concatenate([mod_p, mod[:, bp:bp + bs]], axis=1).reshape(depth, r // CHUNK, 6, d)

    x = jnp.concatenate([x_prompt.reshape(rp, d), x_sample.reshape(rs, d)], axis=0)

    def ffn(x, i):
        hn = _normmod(x, norm_ffn[i], tbl[i, :, 4], tbl[i, :, 3])
        hid = _mm(hn, [w_gate_b[i], w_up_b[i]], bm=bm, bn=512, out_dtype=BF16, epi=_epi_swiglu, name="ffn_gate_up")
        return _residual_mm(hid, w_down_b[i], x, tbl[i, :, 5], bm=bm, bn=1024, bk=d_ffp // 4, name="ffn_down")

    hn = _normmod(x, norm_mix[0], tbl[0, :, 1], tbl[0, :, 0])
    proj = _mm(hn, [w0_main], bm=bm, bn=768, out_dtype=F32, epi=_epi_plain, name="in_proj0")
    small = _mm(hn, [w0_small], bm=bm, bn=2 * LANE, out_dtype=F32, epi=_epi_plain, name="in_proj0_small")

    mix, h_all, conv_all = _ssd(
        proj, small, state_ssd_conv[0], state_ssd[0].reshape(bs, ssd_width, SSD_STATE), conv_w[0],
        conv_b[0].reshape(1, -1), lane_pad(dt_bias[0]), lane_pad(a_log[0]), lane_pad(d_skip[0]),
        ssd_norm[0].reshape(1, -1), bp=bp, ncp=ncp, bs=bs, mix_width=mix0)

    gr_q = _pad_rope_cols(q_norm_rope[0]).reshape(1, LANE)
    gr_k = _pad_rope_cols(k_norm_rope[0]).reshape(1, LANE)
    cq_block = (off_dt + 0) // Q_LORA
    ckv_block = (off_dt + Q_LORA) // KV_LORA
    q_all = _mla_q(proj, cq_block, q_a_norm[0].reshape(1, -1), w_uq_p, q_norm_nope[0].reshape(1, -1), gr_q,
                   cos_m, sin_m)
    ckv_all, kpe_all = _mla_ckv(proj, small, ckv_block, kv_a_norm[0].reshape(1, -1), gr_k, cos_m, sin_m)
    gk = k_norm_nope[0].reshape(1, -1)
    k_all, v_all = _mla_kvup(ckv_all, kpe_all, w_ukv_p, gk)
    k_past, v_past = _mla_kvup(cache_mla_ckv[0].reshape(bs * past, KV_LORA),
                               _pad_rope_cols(cache_mla_kpe[0].reshape(bs * past, QK_ROPE)), w_ukv_p, gk)
    mix = _attn_prompt(q_all, k_all, v_all, mix, bp=bp, tp=tp)
    mix = _attn_sample(q_all, k_past, v_past, k_all, v_all, mix, bs=bs, ts=ts, past=past, row0=rp)

    x = _residual_mm(mix, w_out0_b, x, tbl[0, :, 2], bm=bm, bn=512, name="out_proj0")
    x = ffn(x, 0)

    hn = _normmod(x, norm_mix[1], tbl[1, :, 1], tbl[1, :, 0])
    hpb = 1024 // ret_kdim
    rope_epi = functools.partial(_epi_rope_qk, head_dim=ret_kdim, k_block0=RET_HEADS // hpb,
                                 k_scale=ret_kdim ** -0.5)
    half = ret_kdim // 2
    qk = _mm(hn, [w1_qk], bm=bm, bn=1024, out_dtype=BF16, epi=rope_epi,
             extras=[(cos_r, (bm, half), lambda i, j: (i, 0)), (sin_r, (bm, half), lambda i, j: (i, 0))],
             name="in_proj1_qk")
    vg = _mm(hn, [w1_vg], bm=bm, bn=1024, out_dtype=BF16, epi=_epi_plain, name="in_proj1_vg")
    o_ret, s_all = _retention(qk, vg, state_ret[0], dmask, qdec, kdec, cdec, bp=bp, ncp=ncp, bs=bs)
    x = _residual_mm(o_ret, w_out1_b, x, tbl[1, :, 2], bm=bm, bn=1024, bk=2048, name="out_proj1")
    x = ffn(x, 1)

    def split(a, shape_p, shape_s, n=rp):
        return a[:n].reshape(shape_p), a[n:].reshape(shape_s)

    y_p, y_s = split(x, (bp, tp, d), (bs, ts, d))
    ckv_p, ckv_s = split(ckv_all, (1, bp, tp, KV_LORA), (1, bs, ts, KV_LORA))
    kpe_p, kpe_s = split(_unpad_rope_cols(kpe_all), (1, bp, tp, QK_ROPE), (1, bs, ts, QK_ROPE))
    hshape = (ssd_heads, SSD_HEADDIM, SSD_STATE)
    ssd_p, ssd_s = split(h_all, (1, bp) + hshape, (1, bs) + hshape, bp)
    conv_tail = conv_all[:, 8 - (CONV_W - 1):]
    conv_p, conv_s = split(conv_tail, (1, bp, CONV_W - 1, conv_ch), (1, bs, CONV_W - 1, conv_ch), bp)
    ret_p, ret_s = split(s_all, (1, bp) + s_all.shape[1:], (1, bs) + s_all.shape[1:], bp)
    return (y_p, y_s, ckv_p, kpe_p, ssd_p, conv_p, ret_p, ckv_s, kpe_s, ssd_s, conv_s, ret_s)
```

```python
import functools
import math

import jax
import jax.numpy as jnp
from jax import lax
from jax.experimental import pallas as pl
from jax.experimental.pallas import tpu as pltpu

F32 = jnp.float32
BF16 = jnp.bfloat16

CHUNK = 64
EPS = 1e-6
NEG_BIG = -1e30

SSD_HEADDIM = 64
SSD_GROUPS = 8
SSD_STATE = 128
CONV_W = 4
MLA_HEADS = 16
QK_NOPE = 128
QK_ROPE = 64
V_DIM = 128
Q_LORA = 1024
KV_LORA = 512
ROPE_THETA = 10000.0
MLA_SCALE = (QK_NOPE + QK_ROPE) ** -0.5
Q_PRESCALE = MLA_SCALE * math.log2(math.e)
RET_HEADS = 16
LANE = 128
FF_ALIGN = 1024

VMEM_LIMIT = 56 * 1024 * 1024


def _cparams(sem):
    return pltpu.CompilerParams(dimension_semantics=sem, vmem_limit_bytes=VMEM_LIMIT)


def _sigmoid(x):
    return 1.0 / (1.0 + jnp.exp(-x))


def _silu(x):
    return x * _sigmoid(x)


def _softplus(x):
    return jnp.maximum(x, 0.0) + jnp.log1p(jnp.exp(-jnp.abs(x)))


def _dot(a, b):
    return jnp.dot(a, b, preferred_element_type=F32)


def _dot_nt(a, b):
    return lax.dot_general(a, b, (((1,), (1,)), ((), ())), preferred_element_type=F32)


def _dot_tn(a, b):
    return lax.dot_general(a, b, (((0,), (0,)), ((), ())), preferred_element_type=F32)


def _ada_kernel(c_ref, w_ref, b_ref, o_ref):
    c = c_ref[...]
    o_ref[...] = _dot(_silu(c).astype(BF16), w_ref[...].astype(BF16)) + b_ref[...]


def _ada(c_all, w_ada, b_ada):
    depth, d, n = w_ada.shape
    nb = c_all.shape[0]
    tn = 512
    return pl.pallas_call(
        _ada_kernel,
        out_shape=jax.ShapeDtypeStruct((depth, nb, n), F32),
        grid=(depth, n // tn),
        in_specs=[
            pl.BlockSpec((nb, d), lambda l, j: (0, 0)),
            pl.BlockSpec((None, d, tn), lambda l, j: (l, 0, j)),
            pl.BlockSpec((None, 1, tn), lambda l, j: (l, 0, j)),
        ],
        out_specs=pl.BlockSpec((None, nb, tn), lambda l, j: (l, 0, j)),
        compiler_params=_cparams(("arbitrary", "arbitrary")),
        name="ada",
    )(c_all, w_ada, b_ada.reshape(depth, 1, n))


def _normmod_kernel(x_ref, g_ref, sc_ref, sh_ref, o_ref):
    x = x_ref[...]
    rb, d = x.shape
    y = x * lax.rsqrt(jnp.mean(x * x, axis=-1, keepdims=True) + EPS) * g_ref[...]
    y = y.reshape(rb // CHUNK, CHUNK, d)
    y = y * (1.0 + sc_ref[...][:, None, :]) + sh_ref[...][:, None, :]
    o_ref[...] = y.reshape(rb, d).astype(o_ref.dtype)


def _normmod(x, gain, sc_tbl, sh_tbl):
    r, d = x.shape
    rb = 512
    gb = rb // CHUNK
    return pl.pallas_call(
        _normmod_kernel,
        out_shape=jax.ShapeDtypeStruct((r, d), BF16),
        grid=(r // rb,),
        in_specs=[
            pl.BlockSpec((rb, d), lambda i: (i, 0)),
            pl.BlockSpec((1, d), lambda i: (0, 0)),
            pl.BlockSpec((gb, d), lambda i: (i, 0)),
            pl.BlockSpec((gb, d), lambda i: (i, 0)),
        ],
        out_specs=pl.BlockSpec((rb, d), lambda i: (i, 0)),
        compiler_params=_cparams(("arbitrary",)),
        name="normmod",
    )(x, gain.reshape(1, d), sc_tbl, sh_tbl)


def _mm_kernel(*refs, n_w, n_ex, nk, epi):
    x_ref = refs[0]
    w_refs = refs[1:1 + n_w]
    ex_refs = refs[1 + n_w:1 + n_w + n_ex]
    o_ref = refs[1 + n_w + n_ex]
    acc_refs = refs[2 + n_w + n_ex:]
    j = pl.program_id(1)
    if nk == 1:
        accs = [_dot(x_ref[...], w[...]) for w in w_refs]
        o_ref[...] = epi(accs, ex_refs, j).astype(o_ref.dtype)
    else:
        k = pl.program_id(2)

        @pl.when(k == 0)
        def _():
            for a in acc_refs:
                a[...] = jnp.zeros_like(a)

        for a, w in zip(acc_refs, w_refs):
            a[...] += _dot(x_ref[...], w[...])

        @pl.when(k == nk - 1)
        def _():
            o_ref[...] = epi([a[...] for a in acc_refs], ex_refs, j).astype(o_ref.dtype)


def _mm(x, ws, *, bm, bn, bk=None, out_dtype, epi, extras=(), name):
    m, kdim = x.shape
    n = ws[0].shape[1]
    bk = kdim if bk is None else bk
    nk = kdim // bk
    in_specs = [pl.BlockSpec((bm, bk), lambda i, j, k: (i, k))]
    in_specs += [pl.BlockSpec((bk, bn), lambda i, j, k: (k, j)) for _ in ws]
    for _, bs, im in extras:
        in_specs.append(pl.BlockSpec(bs, functools.partial(lambda i, j, k, im: im(i, j), im=im)))
    scratch = [pltpu.VMEM((bm, bn), F32) for _ in ws] if nk > 1 else []
    return pl.pallas_call(
        functools.partial(_mm_kernel, n_w=len(ws), n_ex=len(extras), nk=nk, epi=epi),
        out_shape=jax.ShapeDtypeStruct((m, n), out_dtype),
        grid=(m // bm, n // bn, nk),
        in_specs=in_specs,
        out_specs=pl.BlockSpec((bm, bn), lambda i, j, k: (i, j)),
        scratch_shapes=scratch,
        compiler_params=_cparams(("arbitrary", "arbitrary", "arbitrary")),
        name=name,
    )(x, *ws, *[e[0] for e in extras])


def _epi_plain(accs, ex, j):
    return accs[0]


def _epi_swiglu(accs, ex, j):
    return _silu(accs[0]) * accs[1]


def _epi_residual(accs, ex, j):
    res_ref, gate_ref = ex
    acc = accs[0]
    bm, bn = acc.shape
    upd = acc.reshape(bm // CHUNK, CHUNK, bn) * gate_ref[...][:, None, :]
    return res_ref[...] + upd.reshape(bm, bn)


def _epi_rope_qk(accs, ex, j, *, head_dim, k_block0, k_scale):
    cos_ref, sin_ref = ex
    acc = accs[0]
    c, s = cos_ref[...], sin_ref[...]
    half = head_dim // 2
    scale = jnp.where(j >= k_block0, k_scale, 1.0).astype(F32)
    outs = []
    for h in range(acc.shape[1] // head_dim):
        x1 = acc[:, h * head_dim:h * head_dim + half]
        x2 = acc[:, h * head_dim + half:(h + 1) * head_dim]
        outs.append((x1 * c - x2 * s) * scale)
        outs.append((x1 * s + x2 * c) * scale)
    return jnp.concatenate(outs, axis=-1)


def _residual_mm(x, w, res, gate_tbl, *, bm, bn, bk=None, name):
    gb = bm // CHUNK
    return _mm(x, [w], bm=bm, bn=bn, bk=bk, out_dtype=F32, epi=_epi_residual,
               extras=[(res, (bm, bn), lambda i, j: (i, j)), (gate_tbl, (gb, bn), lambda i, j: (i, j))],
               name=name)


def _ssd_kernel(*refs, has_init, has_alias):
    refs = list(refs)
    z_ref, xa_ref, xb_ref, sm_ref = refs[:4]
    del refs[:4]
    if has_init:
        cinit_ref, h0_ref = refs[:2]
        del refs[:2]
    cw_ref, cb_ref, dtb_ref, alog_ref, dsk_ref, nrm_ref = refs[:6]
    del refs[:6]
    if has_alias:
        del refs[:1]
    y_ref, hout_ref, cout_ref, ext, xc, ysc = refs
    L, width = xa_ref.shape
    heads = width // SSD_HEADDIM
    rpg = heads // SSD_GROUPS
    gw = rpg * SSD_HEADDIM

    @pl.when(pl.program_id(1) == 0)
    def _():
        if has_init:
            ext[5:8, :] = cinit_ref[...]
            hout_ref[...] = h0_ref[...]
        else:
            ext[0:8, :] = jnp.zeros((8, ext.shape[1]), F32)
            hout_ref[...] = jnp.zeros_like(hout_ref)

    ext[8:8 + L, 0:width] = xa_ref[...]
    ext[8:8 + L, width:2 * width] = xb_ref[...]

    cstep = 512
    for c in range(2 * width // cstep):
        sl = slice(c * cstep, (c + 1) * cstep)
        a = cb_ref[:, sl] + cw_ref[0:1, sl] * ext[5:5 + L, sl]
        for w in range(1, CONV_W):
            a = a + cw_ref[w:w + 1, sl] * ext[5 + w:5 + w + L, sl]
        xc[:, sl] = _silu(a)
    tail_rows = ext[L:L + 8, :]
    cout_ref[...] = tail_rows
    ext[0:8, :] = tail_rows

    dt = _softplus(sm_ref[:, 0:LANE] + dtb_ref[...])
    a_neg = -jnp.exp(alog_ref[...])
    cs = dt * a_neg
    row = lax.broadcasted_iota(jnp.int32, (L, LANE), 0)
    sh = 1
    while sh < L:
        cs = cs + jnp.where(row >= sh, pltpu.roll(cs, sh, axis=0), 0.0)
        sh *= 2
    both_t = jnp.concatenate([cs, dt], axis=0).T
    cs_t = both_t[:, 0:L]
    dt_t = both_t[:, L:2 * L]
    tail_t = jnp.exp(cs_t[:, L - 1:L] - cs_t) * dt_t
    ecs = jnp.exp(cs)

    ii = lax.broadcasted_iota(jnp.int32, (L, L), 0)
    jj = lax.broadcasted_iota(jnp.int32, (L, L), 1)
    causal = ii >= jj

    half = width // 2
    x_t = jnp.concatenate([xc[:, 0:half], xc[:, half:width]], axis=0).T

    for g in range(SSD_GROUPS):
        bg = xc[:, width + g * SSD_STATE:width + (g + 1) * SSD_STATE].astype(BF16)
        cg = xc[:, width + SSD_GROUPS * SSD_STATE + g * SSD_STATE:
                width + SSD_GROUPS * SSD_STATE + (g + 1) * SSD_STATE].astype(BF16)
        cbm = _dot_nt(cg, bg)
        hg = hout_ref[g * gw:(g + 1) * gw, :]
        ystate = _dot_nt(cg, hg.astype(BF16))
        lo = (g * gw) % half
        lanes = slice(0, L) if g * gw < half else slice(L, 2 * L)
        xs_rows, dec_rows = [], []
        for r in range(rpg):
            h = g * rpg + r
            cs_col = cs[:, h:h + 1]
            seg = cs_col - cs_t[h:h + 1, :]
            dec = jnp.exp(jnp.where(causal, seg, NEG_BIG))
            wts = cbm * dec * dt_t[h:h + 1, :]
            xh = xc[:, h * SSD_HEADDIM:(h + 1) * SSD_HEADDIM]
            yh = _dot(wts.astype(BF16), xh.astype(BF16))
            yh = yh + ystate[:, r * SSD_HEADDIM:(r + 1) * SSD_HEADDIM] * ecs[:, h:h + 1]
            yh = yh + xh * dsk_ref[:, h:h + 1]
            ysc[:, h * SSD_HEADDIM:(h + 1) * SSD_HEADDIM] = yh
            xs_rows.append(x_t[lo + r * SSD_HEADDIM:lo + (r + 1) * SSD_HEADDIM, lanes] * tail_t[h:h + 1, :])
            dec_rows.append(jnp.broadcast_to(jnp.exp(cs_t[h:h + 1, L - 1:L]), (SSD_HEADDIM, SSD_STATE)))
        upd = _dot(jnp.concatenate(xs_rows, axis=0).astype(BF16), bg)
        hout_ref[g * gw:(g + 1) * gw, :] = hg * jnp.concatenate(dec_rows, axis=0) + upd

    y = ysc[...] * _silu(z_ref[...])
    outs = []
    for g in range(SSD_GROUPS):
        yg = y[:, g * gw:(g + 1) * gw]
        ms = jnp.mean(yg * yg, axis=-1, keepdims=True)
        outs.append(yg * lax.rsqrt(ms + EPS) * nrm_ref[:, g * gw:(g + 1) * gw])
    y_ref[...] = jnp.concatenate(outs, axis=-1).astype(y_ref.dtype)


def _ssd(proj, small, params, *, L, nseq, ncs, row0, mix_width, init=None, mix=None):
    conv_w, conv_b, dtb, alog, dsk, nrm = params
    r = proj.shape[0]
    width = nrm.shape[1]
    rb0 = row0 // L
    rowblk = lambda s, c: rb0 + s * ncs + c
    const = lambda s, c: (0, 0)
    in_specs = [
        pl.BlockSpec((L, width), lambda s, c: (rowblk(s, c), 0)),
        pl.BlockSpec((L, width), lambda s, c: (rowblk(s, c), 1)),
        pl.BlockSpec((L, width), lambda s, c: (rowblk(s, c), 2)),
        pl.BlockSpec((L, 2 * LANE), lambda s, c: (rowblk(s, c), 0)),
    ]
    args = [proj, proj, proj, small]
    if init is not None:
        in_specs += [pl.BlockSpec((None, CONV_W - 1, 2 * width), lambda s, c: (s, 0, 0)),
                     pl.BlockSpec((None, width, SSD_STATE), lambda s, c: (s, 0, 0))]
        args += list(init)
    in_specs += [pl.BlockSpec((CONV_W, 2 * width), const), pl.BlockSpec((1, 2 * width), const),
                 pl.BlockSpec((1, LANE), const), pl.BlockSpec((1, LANE), const), pl.BlockSpec((1, LANE), const),
                 pl.BlockSpec((1, width), const)]
    args += [conv_w, conv_b, dtb, alog, dsk, nrm]
    aliases = {}
    if mix is not None:
        in_specs.append(pl.BlockSpec(memory_space=pl.ANY))
        aliases = {len(args): 0}
        args.append(mix)
    return pl.pallas_call(
        functools.partial(_ssd_kernel, has_init=init is not None, has_alias=mix is not None),
        out_shape=(jax.ShapeDtypeStruct((r, mix_width), BF16),
                   jax.ShapeDtypeStruct((nseq, width, SSD_STATE), F32),
                   jax.ShapeDtypeStruct((nseq, 8, 2 * width), F32)),
        grid=(nseq, ncs),
        in_specs=in_specs,
        out_specs=(
            pl.BlockSpec((L, width), lambda s, c: (rowblk(s, c), 0)),
            pl.BlockSpec((None, width, SSD_STATE), lambda s, c: (s, 0, 0)),
            pl.BlockSpec((None, 8, 2 * width), lambda s, c: (s, 0, 0)),
        ),
        scratch_shapes=[pltpu.VMEM((L + 8, 2 * width), F32), pltpu.VMEM((L, 2 * width), F32),
                        pltpu.VMEM((L, width), F32)],
        input_output_aliases=aliases,
        compiler_params=_cparams(("arbitrary", "arbitrary")),
        name="ssd",
    )(*args)


def _rope_pad(x, cos_t, sin_t):
    return x * cos_t + pltpu.roll(x, LANE // 2, axis=1) * sin_t


def _mla_q_kernel(cq_ref, ga_ref, w_ref, gn_ref, gr_ref, cos_ref, sin_ref, q_ref):
    cq = cq_ref[...]
    xn = cq * lax.rsqrt(jnp.mean(cq * cq, axis=-1, keepdims=True) + EPS) * ga_ref[...]
    q = _dot(xn.astype(BF16), w_ref[...])
    c, s = cos_ref[...], sin_ref[...]
    nope_w = MLA_HEADS * QK_NOPE
    for h in range(MLA_HEADS):
        qn = q[:, h * QK_NOPE:(h + 1) * QK_NOPE]
        qn = qn * lax.rsqrt(jnp.mean(qn * qn, axis=-1, keepdims=True) + EPS) * gn_ref[...]
        qp = q[:, nope_w + h * LANE:nope_w + (h + 1) * LANE]
        qp = qp * lax.rsqrt(jnp.sum(qp * qp, axis=-1, keepdims=True) * (1.0 / QK_ROPE) + EPS) * gr_ref[...]
        qp = _rope_pad(qp, c, s)
        q_ref[h] = (jnp.concatenate([qn, qp], axis=-1) * Q_PRESCALE).astype(q_ref.dtype)


def _mla_q(proj, cq_block, ga, w_uq_p, gn, gr_pad, cos_t, sin_t):
    r = proj.shape[0]
    bm = 512
    n = w_uq_p.shape[1]
    return pl.pallas_call(
        _mla_q_kernel,
        out_shape=jax.ShapeDtypeStruct((MLA_HEADS, r, 2 * LANE), BF16),
        grid=(r // bm,),
        in_specs=[
            pl.BlockSpec((bm, Q_LORA), lambda i: (i, cq_block)),
            pl.BlockSpec((1, Q_LORA), lambda i: (0, 0)),
            pl.BlockSpec((Q_LORA, n), lambda i: (0, 0)),
            pl.BlockSpec((1, QK_NOPE), lambda i: (0, 0)),
            pl.BlockSpec((1, LANE), lambda i: (0, 0)),
            pl.BlockSpec((bm, LANE), lambda i: (i, 0)),
            pl.BlockSpec((bm, LANE), lambda i: (i, 0)),
        ],
        out_specs=pl.BlockSpec((MLA_HEADS, bm, 2 * LANE), lambda i: (0, i, 0)),
        compiler_params=_cparams(("arbitrary",)),
        name="mla_q",
    )(proj, ga, w_uq_p, gn, gr_pad, cos_t, sin_t)


def _mla_ckv_kernel(ckv_ref, sm_ref, gkv_ref, gr_ref, cos_ref, sin_ref, ckv_out, kpe_out):
    x = ckv_ref[...]
    ckv_out[...] = x * lax.rsqrt(jnp.mean(x * x, axis=-1, keepdims=True) + EPS) * gkv_ref[...]
    kp = sm_ref[:, LANE:2 * LANE]
    kp = kp * lax.rsqrt(jnp.sum(kp * kp, axis=-1, keepdims=True) * (1.0 / QK_ROPE) + EPS) * gr_ref[...]
    kpe_out[...] = _rope_pad(kp, cos_ref[...], sin_ref[...])


def _mla_ckv(proj, small, ckv_block, gkv, gr_pad, cos_t, sin_t):
    r = proj.shape[0]
    bm = 512
    return pl.pallas_call(
        _mla_ckv_kernel,
        out_shape=(jax.ShapeDtypeStruct((r, KV_LORA), F32), jax.ShapeDtypeStruct((r, LANE), F32)),
        grid=(r // bm,),
        in_specs=[
            pl.BlockSpec((bm, KV_LORA), lambda i: (i, ckv_block)),
            pl.BlockSpec((bm, 2 * LANE), lambda i: (i, 0)),
            pl.BlockSpec((1, KV_LORA), lambda i: (0, 0)),
            pl.BlockSpec((1, LANE), lambda i: (0, 0)),
            pl.BlockSpec((bm, LANE), lambda i: (i, 0)),
            pl.BlockSpec((bm, LANE), lambda i: (i, 0)),
        ],
        out_specs=(pl.BlockSpec((bm, KV_LORA), lambda i: (i, 0)), pl.BlockSpec((bm, LANE), lambda i: (i, 0))),
        compiler_params=_cparams(("arbitrary",)),
        name="mla_ckv",
    )(proj, small, gkv, gr_pad, cos_t, sin_t)


def _mla_kvup_kernel(ckv_ref, kpe_ref, w_ref, gk_ref, k_ref, v_ref):
    kv = _dot(ckv_ref[...].astype(BF16), w_ref[...])
    kpe = kpe_ref[...]
    nope_w = MLA_HEADS * QK_NOPE
    for h in range(MLA_HEADS):
        kn = kv[:, h * QK_NOPE:(h + 1) * QK_NOPE]
        kn = kn * lax.rsqrt(jnp.mean(kn * kn, axis=-1, keepdims=True) + EPS) * gk_ref[...]
        k_ref[h] = jnp.concatenate([kn, kpe], axis=-1).astype(k_ref.dtype)
        v_ref[h] = kv[:, nope_w + h * V_DIM:nope_w + (h + 1) * V_DIM].astype(v_ref.dtype)


def _mla_kvup(ckv, kpe_pad, w_ukv_p, gk):
    r = ckv.shape[0]
    bm = 512
    n = w_ukv_p.shape[1]
    return pl.pallas_call(
        _mla_kvup_kernel,
        out_shape=(jax.ShapeDtypeStruct((MLA_HEADS, r, 2 * LANE), BF16),
                   jax.ShapeDtypeStruct((MLA_HEADS, r, V_DIM), BF16)),
        grid=(r // bm,),
        in_specs=[
            pl.BlockSpec((bm, KV_LORA), lambda i: (i, 0)),
            pl.BlockSpec((bm, LANE), lambda i: (i, 0)),
            pl.BlockSpec((KV_LORA, n), lambda i: (0, 0)),
            pl.BlockSpec((1, QK_NOPE), lambda i: (0, 0)),
        ],
        out_specs=(pl.BlockSpec((MLA_HEADS, bm, 2 * LANE), lambda i: (0, i, 0)),
                   pl.BlockSpec((MLA_HEADS, bm, V_DIM), lambda i: (0, i, 0))),
        compiler_params=_cparams(("arbitrary",)),
        name="mla_kvup",
    )(ckv, kpe_pad, w_ukv_p, gk)


def _lane_fold(x, op):
    out = x[:, 0:LANE]
    for c in range(1, x.shape[1] // LANE):
        out = op(out, x[:, c * LANE:(c + 1) * LANE])
    return out


def _attn_prompt_kernel(q_ref, k_ref, v_ref, mix_ref, o_ref, s_scr, *, tq):
    del mix_ref
    tp = q_ref.shape[0]
    ri = lax.broadcasted_iota(jnp.int32, (tq, tq), 0) // CHUNK
    ci = lax.broadcasted_iota(jnp.int32, (tq, tq), 1) // CHUNK
    diag_visible = ci <= ri

    for qi in range(tp // tq):
        q = q_ref[qi * tq:(qi + 1) * tq, :]

        def scores(kb, macc):
            start = pl.multiple_of(kb * tq, tq)
            s = _dot_nt(q, k_ref[pl.ds(start, tq), :])
            s_scr[kb] = s
            return jnp.maximum(macc, _lane_fold(s, jnp.maximum))

        macc = lax.fori_loop(0, qi, scores, jnp.full((tq, LANE), NEG_BIG, F32))
        s = _dot_nt(q, k_ref[qi * tq:(qi + 1) * tq, :])
        s = jnp.where(diag_visible, s, NEG_BIG)
        s_scr[qi] = s
        macc = jnp.maximum(macc, _lane_fold(s, jnp.maximum))
        m = jnp.max(macc, axis=-1, keepdims=True)

        def weights(kb, carry):
            lacc, acc = carry
            start = pl.multiple_of(kb * tq, tq)
            p = jnp.exp2(s_scr[kb] - m)
            return lacc + _lane_fold(p, jnp.add), acc + _dot(p.astype(BF16), v_ref[pl.ds(start, tq), :])

        lacc, acc = lax.fori_loop(0, qi + 1, weights,
                                  (jnp.zeros((tq, LANE), F32), jnp.zeros((tq, V_DIM), F32)))
        l = jnp.sum(lacc, axis=-1, keepdims=True)
        o_ref[qi * tq:(qi + 1) * tq, :] = (acc / l).astype(o_ref.dtype)


def _attn_prompt(q, k, v, mix, *, bp, tp):
    tq = min(512, tp)
    r, mw = mix.shape
    col0 = (mw // 2) // V_DIM
    return pl.pallas_call(
        functools.partial(_attn_prompt_kernel, tq=tq),
        out_shape=jax.ShapeDtypeStruct((r, mw), mix.dtype),
        grid=(bp, MLA_HEADS),
        in_specs=[
            pl.BlockSpec((None, tp, 2 * LANE), lambda b, h: (h, b, 0)),
            pl.BlockSpec((None, tp, 2 * LANE), lambda b, h: (h, b, 0)),
            pl.BlockSpec((None, tp, V_DIM), lambda b, h: (h, b, 0)),
            pl.BlockSpec(memory_space=pl.ANY),
        ],
        out_specs=pl.BlockSpec((tp, V_DIM), lambda b, h: (b, col0 + h)),
        scratch_shapes=[pltpu.VMEM((tp // tq, tq, tq), F32)],
        input_output_aliases={3: 0},
        compiler_params=_cparams(("arbitrary", "arbitrary")),
        name="attn_prompt",
    )(q, k, v, mix)


def _attn_sample_kernel(q_ref, kp_ref, vp_ref, kn_ref, vn_ref, mix_ref, o_ref, *, past):
    del mix_ref
    q = q_ref[...]
    ts = q.shape[0]
    sp = _dot_nt(q, kp_ref[...])
    sn = _dot_nt(q, kn_ref[...])
    row = past + lax.broadcasted_iota(jnp.int32, (ts, ts), 0)
    col = past + lax.broadcasted_iota(jnp.int32, (ts, ts), 1)
    sn = jnp.where(col < (row // CHUNK + 1) * CHUNK, sn, NEG_BIG)
    m = jnp.maximum(jnp.max(sp, axis=-1, keepdims=True), jnp.max(sn, axis=-1, keepdims=True))
    pp = jnp.exp2(sp - m)
    pn = jnp.exp2(sn - m)
    l = jnp.sum(pp, axis=-1, keepdims=True) + jnp.sum(pn, axis=-1, keepdims=True)
    o = _dot(pp.astype(BF16), vp_ref[...]) + _dot(pn.astype(BF16), vn_ref[...])
    o_ref[...] = (o / l).astype(o_ref.dtype)


def _attn_sample(q, k_past, v_past, k_new, v_new, mix, *, bs, ts, past, row0):
    r, mw = mix.shape
    col0 = (mw // 2) // V_DIM
    g0 = row0 // ts
    return pl.pallas_call(
        functools.partial(_attn_sample_kernel, past=past),
        out_shape=jax.ShapeDtypeStruct((r, mw), mix.dtype),
        grid=(bs, MLA_HEADS),
        in_specs=[
            pl.BlockSpec((None, ts, 2 * LANE), lambda b, h: (h, g0 + b, 0)),
            pl.BlockSpec((None, past, 2 * LANE), lambda b, h: (h, b, 0)),
            pl.BlockSpec((None, past, V_DIM), lambda b, h: (h, b, 0)),
            pl.BlockSpec((None, ts, 2 * LANE), lambda b, h: (h, g0 + b, 0)),
            pl.BlockSpec((None, ts, V_DIM), lambda b, h: (h, g0 + b, 0)),
            pl.BlockSpec(memory_space=pl.ANY),
        ],
        out_specs=pl.BlockSpec((ts, V_DIM), lambda b, h: (g0 + b, col0 + h)),
        input_output_aliases={5: 0},
        compiler_params=_cparams(("arbitrary", "arbitrary")),
        name="attn_sample",
    )(q, k_past, v_past, k_new, v_new, mix)


def _ret_kernel(*refs, has_init, has_alias):
    refs = list(refs)
    q_ref, k_ref, v_ref, g_ref = refs[:4]
    del refs[:4]
    if has_init:
        s0_ref = refs.pop(0)
    dm_ref, qd_ref, kd_ref, cd_ref = refs[:4]
    del refs[:4]
    if has_alias:
        del refs[:1]
    o_ref, s_ref = refs
    hpg, kdim, vdim = s_ref.shape

    @pl.when(pl.program_id(2) == 0)
    def _():
        if has_init:
            s_ref[...] = s0_ref[...]
        else:
            s_ref[...] = jnp.zeros_like(s_ref)

    for h in range(hpg):
        q = q_ref[:, h * kdim:(h + 1) * kdim]
        k = k_ref[:, h * kdim:(h + 1) * kdim]
        v = v_ref[:, h * vdim:(h + 1) * vdim]
        s = s_ref[h]
        att = _dot_nt(q, k) * dm_ref[h]
        o = _dot(att.astype(BF16), v) + _dot(q, s.astype(BF16)) * qd_ref[:, h:h + 1]
        kdec = (k.astype(F32) * kd_ref[:, h:h + 1]).astype(BF16)
        s_ref[h] = s * cd_ref[:, h:h + 1] + _dot_tn(kdec, v)
        mu = jnp.mean(o, axis=-1, keepdims=True)
        oc = o - mu
        var = jnp.mean(oc * oc, axis=-1, keepdims=True)
        g = g_ref[:, h * vdim:(h + 1) * vdim].astype(F32)
        o_ref[:, h * vdim:(h + 1) * vdim] = (_silu(g) * (oc * lax.rsqrt(var + EPS))).astype(o_ref.dtype)


def _retention_tables(heads, hpg, L):
    lg = jnp.log1p(-jnp.exp2(-5.0 - jnp.arange(heads, dtype=F32)))
    idx = jnp.arange(L, dtype=F32)
    rel = idx[:, None] - idx[None, :]
    dmask = jnp.exp(jnp.where(rel[None] >= 0, rel[None] * lg[:, None, None], -jnp.inf))
    grp = lambda a: a.reshape(a.shape[0], heads // hpg, hpg).transpose(1, 0, 2)
    qdec = grp(jnp.exp((idx[:, None] + 1.0) * lg[None, :]))
    kdec = grp(jnp.exp((L - 1.0 - idx[:, None]) * lg[None, :]))
    cdec = grp(jnp.exp(L * lg)[None, :])
    return dmask, qdec, kdec, cdec


def _retention(qk, vg, *, L, nseq, ncs, row0, heads, s0=None, o_prev=None):
    r = qk.shape[0]
    kdim = qk.shape[1] // (2 * heads)
    vdim = vg.shape[1] // (2 * heads)
    hpg = 4
    nhg = heads // hpg
    dmask, qdec, kdec, cdec = _retention_tables(heads, hpg, L)
    rb0 = row0 // L
    rowblk = lambda s, c: rb0 + s * ncs + c
    in_specs = [
        pl.BlockSpec((L, hpg * kdim), lambda hg, s, c: (rowblk(s, c), hg)),
        pl.BlockSpec((L, hpg * kdim), lambda hg, s, c: (rowblk(s, c), nhg + hg)),
        pl.BlockSpec((L, hpg * vdim), lambda hg, s, c: (rowblk(s, c), hg)),
        pl.BlockSpec((L, hpg * vdim), lambda hg, s, c: (rowblk(s, c), nhg + hg)),
    ]
    args = [qk, qk, vg, vg]
    if s0 is not None:
        in_specs.append(pl.BlockSpec((None, hpg, kdim, vdim), lambda hg, s, c: (s, hg, 0, 0)))
        args.append(s0)
    in_specs += [pl.BlockSpec((hpg, L, L), lambda hg, s, c: (hg, 0, 0)),
                 pl.BlockSpec((None, L, hpg), lambda hg, s, c: (hg, 0, 0)),
                 pl.BlockSpec((None, L, hpg), lambda hg, s, c: (hg, 0, 0)),
                 pl.BlockSpec((None, 1, hpg), lambda hg, s, c: (hg, 0, 0))]
    args += [dmask, qdec, kdec, cdec]
    aliases = {}
    if o_prev is not None:
        in_specs.append(pl.BlockSpec(memory_space=pl.ANY))
        aliases = {len(args): 0}
        args.append(o_prev)
    return pl.pallas_call(
        functools.partial(_ret_kernel, has_init=s0 is not None, has_alias=o_prev is not None),
        out_shape=(jax.ShapeDtypeStruct((r, heads * vdim), BF16),
                   jax.ShapeDtypeStruct((nseq, heads, kdim, vdim), F32)),
        grid=(nhg, nseq, ncs),
        in_specs=in_specs,
        out_specs=(pl.BlockSpec((L, hpg * vdim), lambda hg, s, c: (rowblk(s, c), hg)),
                   pl.BlockSpec((None, hpg, kdim, vdim), lambda hg, s, c: (s, hg, 0, 0))),
        input_output_aliases=aliases,
        compiler_params=_cparams(("arbitrary", "arbitrary", "arbitrary")),
        name="retention",
    )(*args)


def _pad_rope_cols(a):
    half = QK_ROPE // 2
    z = jnp.zeros(a.shape[:-1] + (LANE // 2 - half,), a.dtype)
    return jnp.concatenate([a[..., :half], z, a[..., half:], z], axis=-1)


def _unpad_rope_cols(a):
    half = QK_ROPE // 2
    return jnp.concatenate([a[..., :half], a[..., LANE // 2:LANE // 2 + half]], axis=-1)


def _rope_tables(pos, half):
    inv = ROPE_THETA ** (-jnp.arange(half, dtype=F32) / half)
    ang = pos.astype(F32)[:, None] * inv[None, :]
    return jnp.cos(ang), jnp.sin(ang)


@jax.jit
def kernel(x_prompt, x_sample, c_prompt, c_sample, cache_mla_ckv, cache_mla_kpe, state_ssd, state_ssd_conv, state_ret, norm_mix, norm_ffn, w_ada, b_ada, w_in0, conv_w, conv_b, dt_bias, a_log, d_skip, ssd_norm, q_a_norm, w_uq, q_norm_nope, q_norm_rope, kv_a_norm, w_ukv, k_norm_nope, k_norm_rope, w_out0, w_in1, w_out1, w_gate, w_up, w_down):
    bp, tp, d = x_prompt.shape
    bs, ts, _ = x_sample.shape
    past = cache_mla_ckv.shape[2]
    assert ts == CHUNK and tp % CHUNK == 0 and past % CHUNK == 0
    rp, rs = bp * tp, bs * ts
    r = rp + rs
    ncp = tp // CHUNK
    bm = min(1024, r)
    assert r % bm == 0 and bm % 512 == 0

    ssd_width = d // 2
    ssd_heads = ssd_width // SSD_HEADDIM
    conv_ch = ssd_width + 2 * SSD_GROUPS * SSD_STATE
    off_dt = ssd_width + conv_ch
    off_cq = off_dt + ssd_heads
    off_ckv = off_cq + Q_LORA
    off_kpe = off_ckv + KV_LORA
    assert conv_ch == 2 * ssd_width and ssd_heads <= LANE
    mix0 = ssd_width + MLA_HEADS * V_DIM
    ret_kdim = d // RET_HEADS
    ret_qk = RET_HEADS * ret_kdim
    d_ff = w_gate.shape[2]
    d_ffp = -(-d_ff // FF_ALIGN) * FF_ALIGN

    wi = w_in0[0]
    w0_main = jnp.concatenate([wi[:, :off_dt], wi[:, off_cq:off_kpe]], axis=1).astype(BF16)
    w0_small = jnp.concatenate([wi[:, off_dt:off_cq], jnp.zeros((d, LANE - ssd_heads), F32),
                                _pad_rope_cols(wi[:, off_kpe:])], axis=1).astype(BF16)
    wq = w_uq[0].reshape(Q_LORA, MLA_HEADS, QK_NOPE + QK_ROPE)
    w_uq_p = jnp.concatenate([wq[:, :, :QK_NOPE].reshape(Q_LORA, -1),
                              _pad_rope_cols(wq[:, :, QK_NOPE:]).reshape(Q_LORA, -1)], axis=1).astype(BF16)
    wkv = w_ukv[0].reshape(KV_LORA, MLA_HEADS, QK_NOPE + V_DIM)
    w_ukv_p = jnp.concatenate([wkv[:, :, :QK_NOPE].reshape(KV_LORA, -1),
                               wkv[:, :, QK_NOPE:].reshape(KV_LORA, -1)], axis=1).astype(BF16)
    w_out0_b = w_out0[0].astype(BF16)
    w1_qk = w_in1[0][:, :2 * ret_qk].astype(BF16)
    w1_vg = w_in1[0][:, 2 * ret_qk:].astype(BF16)
    w_out1_b = w_out1[0].astype(BF16)
    ffpad = d_ffp - d_ff
    w_gate_b = jnp.pad(w_gate, ((0, 0), (0, 0), (0, ffpad))).astype(BF16)
    w_up_b = jnp.pad(w_up, ((0, 0), (0, 0), (0, ffpad))).astype(BF16)
    w_down_b = jnp.pad(w_down, ((0, 0), (0, ffpad), (0, 0))).astype(BF16)

    def lane_pad(a, n=LANE):
        return jnp.pad(a, (0, n - a.shape[0])).reshape(1, n)

    pos = jnp.concatenate([jnp.tile(jnp.arange(tp), bp), jnp.tile(past + jnp.arange(ts), bs)])
    c32, s32 = _rope_tables(pos, QK_ROPE // 2)
    zq = jnp.zeros_like(c32)
    cos_m = jnp.concatenate([c32, zq, c32, zq], axis=1)
    sin_m = jnp.concatenate([-s32, zq, s32, zq], axis=1)
    cos_r, sin_r = _rope_tables(pos, ret_kdim // 2)
    lp = min(256, tp)

    nb = -(-(bp + bs) // 8) * 8
    c_all = jnp.concatenate([c_prompt, c_sample, jnp.zeros((nb - bp - bs, d), F32)], axis=0)
    mod = _ada(c_all, w_ada, b_ada)
    depth = mod.shape[0]
    mod_p = jnp.broadcast_to(mod[:, :bp, None, :], (depth, bp, ncp, 6 * d)).reshape(depth, bp * ncp, 6 * d)
    tbl = jnp.concatenate([mod_p, mod[:, bp:bp + bs]], axis=1).reshape(depth, r // CHUNK, 6, d)

    x = jnp.concatenate([x_prompt.reshape(rp, d), x_sample.reshape(rs, d)], axis=0)

    def ffn(x, i):
        hn = _normmod(x, norm_ffn[i], tbl[i, :, 4], tbl[i, :, 3])
        hid = _mm(hn, [w_gate_b[i], w_up_b[i]], bm=bm, bn=512, out_dtype=BF16, epi=_epi_swiglu, name="ffn_gate_up")
        return _residual_mm(hid, w_down_b[i], x, tbl[i, :, 5], bm=bm, bn=1024, bk=d_ffp // 4, name="ffn_down")

    hn = _normmod(x, norm_mix[0], tbl[0, :, 1], tbl[0, :, 0])
    proj = _mm(hn, [w0_main], bm=bm, bn=768, out_dtype=F32, epi=_epi_plain, name="in_proj0")
    small = _mm(hn, [w0_small], bm=bm, bn=2 * LANE, out_dtype=F32, epi=_epi_plain, name="in_proj0_small")

    ssd_params = (conv_w[0], conv_b[0].reshape(1, -1), lane_pad(dt_bias[0]), lane_pad(a_log[0]),
                  lane_pad(d_skip[0]), ssd_norm[0].reshape(1, -1))
    mix, h_p, conv_p8 = _ssd(proj, small, ssd_params, L=lp, nseq=bp, ncs=tp // lp, row0=0, mix_width=mix0)
    mix, h_s, conv_s8 = _ssd(proj, small, ssd_params, L=ts, nseq=bs, ncs=1, row0=rp, mix_width=mix0,
                             init=(state_ssd_conv[0], state_ssd[0].reshape(bs, ssd_width, SSD_STATE)), mix=mix)

    gr_q = _pad_rope_cols(q_norm_rope[0]).reshape(1, LANE)
    gr_k = _pad_rope_cols(k_norm_rope[0]).reshape(1, LANE)
    cq_block = (off_dt + 0) // Q_LORA
    ckv_block = (off_dt + Q_LORA) // KV_LORA
    q_all = _mla_q(proj, cq_block, q_a_norm[0].reshape(1, -1), w_uq_p, q_norm_nope[0].reshape(1, -1), gr_q,
                   cos_m, sin_m)
    ckv_all, kpe_all = _mla_ckv(proj, small, ckv_block, kv_a_norm[0].reshape(1, -1), gr_k, cos_m, sin_m)
    gk = k_norm_nope[0].reshape(1, -1)
    k_all, v_all = _mla_kvup(ckv_all, kpe_all, w_ukv_p, gk)
    k_past, v_past = _mla_kvup(cache_mla_ckv[0].reshape(bs * past, KV_LORA),
                               _pad_rope_cols(cache_mla_kpe[0].reshape(bs * past, QK_ROPE)), w_ukv_p, gk)
    mix = _attn_prompt(q_all, k_all, v_all, mix, bp=bp, tp=tp)
    mix = _attn_sample(q_all, k_past, v_past, k_all, v_all, mix, bs=bs, ts=ts, past=past, row0=rp)

    x = _residual_mm(mix, w_out0_b, x, tbl[0, :, 2], bm=bm, bn=512, name="out_proj0")
    x = ffn(x, 0)

    hn = _normmod(x, norm_mix[1], tbl[1, :, 1], tbl[1, :, 0])
    hpb = 1024 // ret_kdim
    rope_epi = functools.partial(_epi_rope_qk, head_dim=ret_kdim, k_block0=RET_HEADS // hpb,
                                 k_scale=ret_kdim ** -0.5)
    half = ret_kdim // 2
    qk = _mm(hn, [w1_qk], bm=bm, bn=1024, out_dtype=BF16, epi=rope_epi,
             extras=[(cos_r, (bm, half), lambda i, j: (i, 0)), (sin_r, (bm, half), lambda i, j: (i, 0))],
             name="in_proj1_qk")
    vg = _mm(hn, [w1_vg], bm=bm, bn=1024, out_dtype=BF16, epi=_epi_plain, name="in_proj1_vg")
    o_ret, ret_p = _retention(qk, vg, L=lp, nseq=bp, ncs=tp // lp, row0=0, heads=RET_HEADS)
    o_ret, ret_s = _retention(qk, vg, L=ts, nseq=bs, ncs=1, row0=rp, heads=RET_HEADS, s0=state_ret[0],
                              o_prev=o_ret)
    x = _residual_mm(o_ret, w_out1_b, x, tbl[1, :, 2], bm=bm, bn=1024, bk=2048, name="out_proj1")
    x = ffn(x, 1)

    def split(a, shape_p, shape_s, n=rp):
        return a[:n].reshape(shape_p), a[n:].reshape(shape_s)

    y_p, y_s = split(x, (bp, tp, d), (bs, ts, d))
    ckv_p, ckv_s = split(ckv_all, (1, bp, tp, KV_LORA), (1, bs, ts, KV_LORA))
    kpe_p, kpe_s = split(_unpad_rope_cols(kpe_all), (1, bp, tp, QK_ROPE), (1, bs, ts, QK_ROPE))
    hshape = (ssd_heads, SSD_HEADDIM, SSD_STATE)
    ssd_p, ssd_s = h_p.reshape((1, bp) + hshape), h_s.reshape((1, bs) + hshape)
    conv_p, conv_s = conv_p8[None, :, 8 - (CONV_W - 1):], conv_s8[None, :, 8 - (CONV_W - 1):]
    return (y_p, y_s, ckv_p, kpe_p, ssd_p, conv_p, ret_p[None], ckv_s, kpe_s, ssd_s, conv_s, ret_s[None])
```

```python
import functools
import math

import jax
import jax.numpy as jnp
from jax import lax
from jax.experimental import pallas as pl
from jax.experimental.pallas import tpu as pltpu

F32 = jnp.float32
BF16 = jnp.bfloat16

CHUNK = 64
EPS = 1e-6
NEG_BIG = -1e30

SSD_HEADDIM = 64
SSD_GROUPS = 8
SSD_STATE = 128
CONV_W = 4
MLA_HEADS = 16
QK_NOPE = 128
QK_ROPE = 64
V_DIM = 128
Q_LORA = 1024
KV_LORA = 512
ROPE_THETA = 10000.0
MLA_SCALE = (QK_NOPE + QK_ROPE) ** -0.5
Q_PRESCALE = MLA_SCALE * math.log2(math.e)
RET_HEADS = 16
LANE = 128

VMEM_LIMIT = 56 * 1024 * 1024


def _cparams(sem):
    return pltpu.CompilerParams(dimension_semantics=sem, vmem_limit_bytes=VMEM_LIMIT)


def _sigmoid(x):
    return 1.0 / (1.0 + jnp.exp(-x))


def _silu(x):
    return x * _sigmoid(x)


def _softplus(x):
    return jnp.maximum(x, 0.0) + jnp.log1p(jnp.exp(-jnp.abs(x)))


def _dot(a, b):
    return jnp.dot(a, b, preferred_element_type=F32)


def _dot_nt(a, b):
    return lax.dot_general(a, b, (((1,), (1,)), ((), ())), preferred_element_type=F32)


def _dot_tn(a, b):
    return lax.dot_general(a, b, (((0,), (0,)), ((), ())), preferred_element_type=F32)


def _ada_kernel(c_ref, w_ref, b_ref, o_ref):
    c = c_ref[...]
    o_ref[...] = _dot(_silu(c).astype(BF16), w_ref[...].astype(BF16)) + b_ref[...]


def _ada(c_all, w_ada, b_ada):
    depth, d, n = w_ada.shape
    nb = c_all.shape[0]
    tn = 512
    return pl.pallas_call(
        _ada_kernel,
        out_shape=jax.ShapeDtypeStruct((depth, nb, n), F32),
        grid=(depth, n // tn),
        in_specs=[
            pl.BlockSpec((nb, d), lambda l, j: (0, 0)),
            pl.BlockSpec((None, d, tn), lambda l, j: (l, 0, j)),
            pl.BlockSpec((None, 1, tn), lambda l, j: (l, 0, j)),
        ],
        out_specs=pl.BlockSpec((None, nb, tn), lambda l, j: (l, 0, j)),
        compiler_params=_cparams(("arbitrary", "arbitrary")),
        name="ada",
    )(c_all, w_ada, b_ada.reshape(depth, 1, n))


def _normmod_kernel(x_ref, g_ref, sc_ref, sh_ref, *rest):
    o_ref = rest[-1]
    x = x_ref[...]
    rb, d = x.shape
    y = x * lax.rsqrt(jnp.mean(x * x, axis=-1, keepdims=True) + EPS) * g_ref[...]
    y = y.reshape(rb // CHUNK, CHUNK, d)
    y = y * (1.0 + sc_ref[...][:, None, :]) + sh_ref[...][:, None, :]
    o_ref[...] = y.reshape(rb, d).astype(o_ref.dtype)


def _normmod(x, gain, sc_tbl, sh_tbl, *, out_rows=None, row0=0, prev=None):
    r, d = x.shape
    rb = 512
    gb = rb // CHUNK
    b0 = row0 // rb
    out_rows = r if out_rows is None else out_rows
    in_specs = [
        pl.BlockSpec((rb, d), lambda i: (i, 0)),
        pl.BlockSpec((1, d), lambda i: (0, 0)),
        pl.BlockSpec((gb, d), lambda i: (b0 + i, 0)),
        pl.BlockSpec((gb, d), lambda i: (b0 + i, 0)),
    ]
    args = [x, gain.reshape(1, d), sc_tbl, sh_tbl]
    aliases = {}
    if prev is not None:
        in_specs.append(pl.BlockSpec(memory_space=pl.ANY))
        args.append(prev)
        aliases = {4: 0}
    return pl.pallas_call(
        _normmod_kernel,
        out_shape=jax.ShapeDtypeStruct((out_rows, d), BF16),
        grid=(r // rb,),
        in_specs=in_specs,
        out_specs=pl.BlockSpec((rb, d), lambda i: (b0 + i, 0)),
        input_output_aliases=aliases,
        compiler_params=_cparams(("arbitrary",)),
        name="normmod",
    )(*args)


def _mm_kernel(*refs, n_w, n_ex, nk, epi, has_alias):
    x_ref = refs[0]
    w_refs = refs[1:1 + n_w]
    ex_refs = refs[1 + n_w:1 + n_w + n_ex]
    refs = refs[1 + n_w + n_ex + (1 if has_alias else 0):]
    o_ref = refs[0]
    acc_refs = refs[1:]
    j = pl.program_id(1)
    if nk == 1:
        accs = [_dot(x_ref[...], w[...]) for w in w_refs]
        o_ref[...] = epi(accs, ex_refs, j).astype(o_ref.dtype)
    else:
        k = pl.program_id(2)

        @pl.when(k == 0)
        def _():
            for a in acc_refs:
                a[...] = jnp.zeros_like(a)

        for a, w in zip(acc_refs, w_refs):
            a[...] += _dot(x_ref[...], w[...])

        @pl.when(k == nk - 1)
        def _():
            o_ref[...] = epi([a[...] for a in acc_refs], ex_refs, j).astype(o_ref.dtype)


def _mm(x, ws, *, bm, bn, bk=None, out_dtype, epi, extras=(), name, n=None, w_col0=0, layer=None, rows=None,
        x_row0=0, out_rows=None, out_row0=0, prev=None):
    kdim = x.shape[1]
    rows = x.shape[0] if rows is None else rows
    n = ws[0].shape[-1] if n is None else n
    out_rows = rows if out_rows is None else out_rows
    bk = kdim if bk is None else bk
    nk = kdim // bk
    xb, ob, wb = x_row0 // bm, out_row0 // bm, w_col0 // bn
    in_specs = [pl.BlockSpec((bm, bk), lambda i, j, k: (xb + i, k))]
    if layer is None:
        in_specs += [pl.BlockSpec((bk, bn), lambda i, j, k: (k, wb + j)) for _ in ws]
    else:
        in_specs += [pl.BlockSpec((None, bk, bn), lambda i, j, k: (layer, k, wb + j)) for _ in ws]
    for _, bs, im in extras:
        in_specs.append(pl.BlockSpec(bs, functools.partial(lambda i, j, k, im: im(i, j), im=im)))
    args = [x, *ws, *[e[0] for e in extras]]
    aliases = {}
    if prev is not None:
        in_specs.append(pl.BlockSpec(memory_space=pl.ANY))
        aliases = {len(args): 0}
        args.append(prev)
    scratch = [pltpu.VMEM((bm, bn), F32) for _ in ws] if nk > 1 else []
    return pl.pallas_call(
        functools.partial(_mm_kernel, n_w=len(ws), n_ex=len(extras), nk=nk, epi=epi, has_alias=prev is not None),
        out_shape=jax.ShapeDtypeStruct((out_rows, n), out_dtype),
        grid=(rows // bm, n // bn, nk),
        in_specs=in_specs,
        out_specs=pl.BlockSpec((bm, bn), lambda i, j, k: (ob + i, j)),
        scratch_shapes=scratch,
        input_output_aliases=aliases,
        compiler_params=_cparams(("arbitrary", "arbitrary", "arbitrary")),
        name=name,
    )(*args)


def _epi_plain(accs, ex, j):
    return accs[0]


def _epi_swiglu(accs, ex, j):
    return _silu(accs[0]) * accs[1]


def _epi_residual(accs, ex, j):
    res_ref, gate_ref = ex
    acc = accs[0]
    bm, bn = acc.shape
    upd = acc.reshape(bm // CHUNK, CHUNK, bn) * gate_ref[...][:, None, :]
    return res_ref[...] + upd.reshape(bm, bn)


def _epi_rope_qk(accs, ex, j, *, head_dim, k_block0, k_scale):
    cos_ref, sin_ref = ex
    acc = accs[0]
    c, s = cos_ref[...], sin_ref[...]
    half = head_dim // 2
    scale = jnp.where(j >= k_block0, k_scale, 1.0).astype(F32)
    outs = []
    for h in range(acc.shape[1] // head_dim):
        x1 = acc[:, h * head_dim:h * head_dim + half]
        x2 = acc[:, h * head_dim + half:(h + 1) * head_dim]
        outs.append((x1 * c - x2 * s) * scale)
        outs.append((x1 * s + x2 * c) * scale)
    return jnp.concatenate(outs, axis=-1)


def _residual_mm(x, w, res, gate_tbl, *, bm, bn, bk=None, name, layer=None, rows=None, x_row0=0, res_row0=0,
                 out_rows=None, out_row0=0, prev=None):
    gb = bm // CHUNK
    xb, rb = x_row0 // bm, res_row0 // bm
    return _mm(x, [w], bm=bm, bn=bn, bk=bk, out_dtype=F32, epi=_epi_residual,
               extras=[(res, (bm, bn), lambda i, j: (rb + i, j)), (gate_tbl, (gb, bn), lambda i, j: (xb + i, j))],
               name=name, layer=layer, rows=rows, x_row0=x_row0, out_rows=out_rows, out_row0=out_row0, prev=prev)


def _ssd_kernel(*refs, has_init, has_alias):
    refs = list(refs)
    z_ref, xa_ref, xb_ref, sm_ref = refs[:4]
    del refs[:4]
    if has_init:
        cinit_ref, h0_ref = refs[:2]
        del refs[:2]
    cw_ref, cb_ref, dtb_ref, alog_ref, dsk_ref, nrm_ref = refs[:6]
    del refs[:6]
    if has_alias:
        del refs[:1]
    y_ref, hout_ref, cout_ref, ext, xc, ysc = refs
    L, width = xa_ref.shape
    heads = width // SSD_HEADDIM
    rpg = heads // SSD_GROUPS
    gw = rpg * SSD_HEADDIM

    @pl.when(pl.program_id(1) == 0)
    def _():
        if has_init:
            ext[5:8, :] = cinit_ref[...]
            hout_ref[...] = h0_ref[...]
        else:
            ext[0:8, :] = jnp.zeros((8, ext.shape[1]), F32)
            hout_ref[...] = jnp.zeros_like(hout_ref)

    ext[8:8 + L, 0:width] = xa_ref[...]
    ext[8:8 + L, width:2 * width] = xb_ref[...]

    cstep = 512
    for c in range(2 * width // cstep):
        sl = slice(c * cstep, (c + 1) * cstep)
        a = cb_ref[:, sl] + cw_ref[0:1, sl] * ext[5:5 + L, sl]
        for w in range(1, CONV_W):
            a = a + cw_ref[w:w + 1, sl] * ext[5 + w:5 + w + L, sl]
        xc[:, sl] = _silu(a)
    tail_rows = ext[L:L + 8, :]
    cout_ref[...] = tail_rows
    ext[0:8, :] = tail_rows

    dt = _softplus(sm_ref[:, 0:LANE] + dtb_ref[...])
    a_neg = -jnp.exp(alog_ref[...])
    cs = dt * a_neg
    row = lax.broadcasted_iota(jnp.int32, (L, LANE), 0)
    sh = 1
    while sh < L:
        cs = cs + jnp.where(row >= sh, pltpu.roll(cs, sh, axis=0), 0.0)
        sh *= 2
    both_t = jnp.concatenate([cs, dt], axis=0).T
    cs_t = both_t[:, 0:L]
    dt_t = both_t[:, L:2 * L]
    tail_t = jnp.exp(cs_t[:, L - 1:L] - cs_t) * dt_t
    ecs = jnp.exp(cs)

    ii = lax.broadcasted_iota(jnp.int32, (L, L), 0)
    jj = lax.broadcasted_iota(jnp.int32, (L, L), 1)
    causal = ii >= jj

    half = width // 2
    x_t = jnp.concatenate([xc[:, 0:half], xc[:, half:width]], axis=0).T

    for g in range(SSD_GROUPS):
        bg = xc[:, width + g * SSD_STATE:width + (g + 1) * SSD_STATE].astype(BF16)
        cg = xc[:, width + SSD_GROUPS * SSD_STATE + g * SSD_STATE:
                width + SSD_GROUPS * SSD_STATE + (g + 1) * SSD_STATE].astype(BF16)
        cbm = _dot_nt(cg, bg)
        hg = hout_ref[g * gw:(g + 1) * gw, :]
        ystate = _dot_nt(cg, hg.astype(BF16))
        lo = (g * gw) % half
        lanes = slice(0, L) if g * gw < half else slice(L, 2 * L)
        xs_rows, dec_rows = [], []
        for r in range(rpg):
            h = g * rpg + r
            cs_col = cs[:, h:h + 1]
            seg = cs_col - cs_t[h:h + 1, :]
            dec = jnp.exp(jnp.where(causal, seg, NEG_BIG))
            wts = cbm * dec * dt_t[h:h + 1, :]
            xh = xc[:, h * SSD_HEADDIM:(h + 1) * SSD_HEADDIM]
            yh = _dot(wts.astype(BF16), xh.astype(BF16))
            yh = yh + ystate[:, r * SSD_HEADDIM:(r + 1) * SSD_HEADDIM] * ecs[:, h:h + 1]
            yh = yh + xh * dsk_ref[:, h:h + 1]
            ysc[:, h * SSD_HEADDIM:(h + 1) * SSD_HEADDIM] = yh
            xs_rows.append(x_t[lo + r * SSD_HEADDIM:lo + (r + 1) * SSD_HEADDIM, lanes] * tail_t[h:h + 1, :])
            dec_rows.append(jnp.broadcast_to(jnp.exp(cs_t[h:h + 1, L - 1:L]), (SSD_HEADDIM, SSD_STATE)))
        upd = _dot(jnp.concatenate(xs_rows, axis=0).astype(BF16), bg)
        hout_ref[g * gw:(g + 1) * gw, :] = hg * jnp.concatenate(dec_rows, axis=0) + upd

    y = ysc[...] * _silu(z_ref[...])
    outs = []
    for g in range(SSD_GROUPS):
        yg = y[:, g * gw:(g + 1) * gw]
        ms = jnp.mean(yg * yg, axis=-1, keepdims=True)
        outs.append(yg * lax.rsqrt(ms + EPS) * nrm_ref[:, g * gw:(g + 1) * gw])
    y_ref[...] = jnp.concatenate(outs, axis=-1).astype(y_ref.dtype)


def _ssd(proj, small, params, *, L, nseq, ncs, row0, mix_width, init=None, mix=None):
    conv_w, conv_b, dtb, alog, dsk, nrm = params
    r = proj.shape[0]
    width = nrm.shape[1]
    rb0 = row0 // L
    rowblk = lambda s, c: rb0 + s * ncs + c
    const = lambda s, c: (0, 0)
    in_specs = [
        pl.BlockSpec((L, width), lambda s, c: (rowblk(s, c), 0)),
        pl.BlockSpec((L, width), lambda s, c: (rowblk(s, c), 1)),
        pl.BlockSpec((L, width), lambda s, c: (rowblk(s, c), 2)),
        pl.BlockSpec((L, 2 * LANE), lambda s, c: (rowblk(s, c), 0)),
    ]
    args = [proj, proj, proj, small]
    if init is not None:
        in_specs += [pl.BlockSpec((None, CONV_W - 1, 2 * width), lambda s, c: (s, 0, 0)),
                     pl.BlockSpec((None, width, SSD_STATE), lambda s, c: (s, 0, 0))]
        args += list(init)
    in_specs += [pl.BlockSpec((CONV_W, 2 * width), const), pl.BlockSpec((1, 2 * width), const),
                 pl.BlockSpec((1, LANE), const), pl.BlockSpec((1, LANE), const), pl.BlockSpec((1, LANE), const),
                 pl.BlockSpec((1, width), const)]
    args += [conv_w, conv_b, dtb, alog, dsk, nrm]
    aliases = {}
    if mix is not None:
        in_specs.append(pl.BlockSpec(memory_space=pl.ANY))
        aliases = {len(args): 0}
        args.append(mix)
    return pl.pallas_call(
        functools.partial(_ssd_kernel, has_init=init is not None, has_alias=mix is not None),
        out_shape=(jax.ShapeDtypeStruct((r, mix_width), BF16),
                   jax.ShapeDtypeStruct((nseq, width, SSD_STATE), F32),
                   jax.ShapeDtypeStruct((nseq, 8, 2 * width), F32)),
        grid=(nseq, ncs),
        in_specs=in_specs,
        out_specs=(
            pl.BlockSpec((L, width), lambda s, c: (rowblk(s, c), 0)),
            pl.BlockSpec((None, width, SSD_STATE), lambda s, c: (s, 0, 0)),
            pl.BlockSpec((None, 8, 2 * width), lambda s, c: (s, 0, 0)),
        ),
        scratch_shapes=[pltpu.VMEM((L + 8, 2 * width), F32), pltpu.VMEM((L, 2 * width), F32),
                        pltpu.VMEM((L, width), F32)],
        input_output_aliases=aliases,
        compiler_params=_cparams(("arbitrary", "arbitrary")),
        name="ssd",
    )(*args)


def _rope_pad(x, cos_t, sin_t):
    return x * cos_t + pltpu.roll(x, LANE // 2, axis=1) * sin_t


def _mla_q_kernel(cq_ref, ga_ref, w_ref, gn_ref, gr_ref, cos_ref, sin_ref, q_ref):
    cq = cq_ref[...]
    xn = cq * lax.rsqrt(jnp.mean(cq * cq, axis=-1, keepdims=True) + EPS) * ga_ref[...]
    q = _dot(xn.astype(BF16), w_ref[...])
    c, s = cos_ref[...], sin_ref[...]
    nope_w = MLA_HEADS * QK_NOPE
    for h in range(MLA_HEADS):
        qn = q[:, h * QK_NOPE:(h + 1) * QK_NOPE]
        qn = qn * lax.rsqrt(jnp.mean(qn * qn, axis=-1, keepdims=True) + EPS) * gn_ref[...]
        qp = q[:, nope_w + h * LANE:nope_w + (h + 1) * LANE]
        qp = qp * lax.rsqrt(jnp.sum(qp * qp, axis=-1, keepdims=True) * (1.0 / QK_ROPE) + EPS) * gr_ref[...]
        qp = _rope_pad(qp, c, s)
        q_ref[h] = (jnp.concatenate([qn, qp], axis=-1) * Q_PRESCALE).astype(q_ref.dtype)


def _mla_q(proj, cq_block, ga, w_uq_p, gn, gr_pad, cos_t, sin_t):
    r = proj.shape[0]
    bm = 512
    n = w_uq_p.shape[1]
    return pl.pallas_call(
        _mla_q_kernel,
        out_shape=jax.ShapeDtypeStruct((MLA_HEADS, r, 2 * LANE), BF16),
        grid=(r // bm,),
        in_specs=[
            pl.BlockSpec((bm, Q_LORA), lambda i: (i, cq_block)),
            pl.BlockSpec((1, Q_LORA), lambda i: (0, 0)),
            pl.BlockSpec((Q_LORA, n), lambda i: (0, 0)),
            pl.BlockSpec((1, QK_NOPE), lambda i: (0, 0)),
            pl.BlockSpec((1, LANE), lambda i: (0, 0)),
            pl.BlockSpec((bm, LANE), lambda i: (i, 0)),
            pl.BlockSpec((bm, LANE), lambda i: (i, 0)),
        ],
        out_specs=pl.BlockSpec((MLA_HEADS, bm, 2 * LANE), lambda i: (0, i, 0)),
        compiler_params=_cparams(("arbitrary",)),
        name="mla_q",
    )(proj, ga, w_uq_p, gn, gr_pad, cos_t, sin_t)


def _mla_ckv_kernel(ckv_ref, sm_ref, gkv_ref, gr_ref, cos_ref, sin_ref, ckv_out, kpe_out):
    x = ckv_ref[...]
    ckv_out[...] = x * lax.rsqrt(jnp.mean(x * x, axis=-1, keepdims=True) + EPS) * gkv_ref[...]
    kp = sm_ref[:, LANE:2 * LANE]
    kp = kp * lax.rsqrt(jnp.sum(kp * kp, axis=-1, keepdims=True) * (1.0 / QK_ROPE) + EPS) * gr_ref[...]
    kpe_out[...] = _rope_pad(kp, cos_ref[...], sin_ref[...])


def _mla_ckv(proj, small, ckv_block, gkv, gr_pad, cos_t, sin_t):
    r = proj.shape[0]
    bm = 512
    return pl.pallas_call(
        _mla_ckv_kernel,
        out_shape=(jax.ShapeDtypeStruct((r, KV_LORA), F32), jax.ShapeDtypeStruct((r, LANE), F32)),
        grid=(r // bm,),
        in_specs=[
            pl.BlockSpec((bm, KV_LORA), lambda i: (i, ckv_block)),
            pl.BlockSpec((bm, 2 * LANE), lambda i: (i, 0)),
            pl.BlockSpec((1, KV_LORA), lambda i: (0, 0)),
            pl.BlockSpec((1, LANE), lambda i: (0, 0)),
            pl.BlockSpec((bm, LANE), lambda i: (i, 0)),
            pl.BlockSpec((bm, LANE), lambda i: (i, 0)),
        ],
        out_specs=(pl.BlockSpec((bm, KV_LORA), lambda i: (i, 0)), pl.BlockSpec((bm, LANE), lambda i: (i, 0))),
        compiler_params=_cparams(("arbitrary",)),
        name="mla_ckv",
    )(proj, small, gkv, gr_pad, cos_t, sin_t)


def _mla_kvup_kernel(ckv_ref, kpe_ref, w_ref, gk_ref, k_ref, v_ref):
    kv = _dot(ckv_ref[...].astype(BF16), w_ref[...])
    kpe = kpe_ref[...]
    nope_w = MLA_HEADS * QK_NOPE
    for h in range(MLA_HEADS):
        kn = kv[:, h * QK_NOPE:(h + 1) * QK_NOPE]
        kn = kn * lax.rsqrt(jnp.mean(kn * kn, axis=-1, keepdims=True) + EPS) * gk_ref[...]
        k_ref[h] = jnp.concatenate([kn, kpe], axis=-1).astype(k_ref.dtype)
        vh = kv[:, nope_w + h * V_DIM:nope_w + (h + 1) * V_DIM]
        v_ref[h] = jnp.concatenate([vh, jnp.ones_like(vh)], axis=-1).astype(v_ref.dtype)


def _mla_kvup(ckv, kpe_pad, w_ukv_p, gk):
    r = ckv.shape[0]
    bm = 512
    n = w_ukv_p.shape[1]
    return pl.pallas_call(
        _mla_kvup_kernel,
        out_shape=(jax.ShapeDtypeStruct((MLA_HEADS, r, 2 * LANE), BF16),
                   jax.ShapeDtypeStruct((MLA_HEADS, r, 2 * V_DIM), BF16)),
        grid=(r // bm,),
        in_specs=[
            pl.BlockSpec((bm, KV_LORA), lambda i: (i, 0)),
            pl.BlockSpec((bm, LANE), lambda i: (i, 0)),
            pl.BlockSpec((KV_LORA, n), lambda i: (0, 0)),
            pl.BlockSpec((1, QK_NOPE), lambda i: (0, 0)),
        ],
        out_specs=(pl.BlockSpec((MLA_HEADS, bm, 2 * LANE), lambda i: (0, i, 0)),
                   pl.BlockSpec((MLA_HEADS, bm, 2 * V_DIM), lambda i: (0, i, 0))),
        compiler_params=_cparams(("arbitrary",)),
        name="mla_kvup",
    )(ckv, kpe_pad, w_ukv_p, gk)


def _lane_fold(x, op):
    out = x[:, 0:LANE]
    for c in range(1, x.shape[1] // LANE):
        out = op(out, x[:, c * LANE:(c + 1) * LANE])
    return out


def _attn_prompt_kernel(q_ref, k_ref, v_ref, mix_ref, o_ref, s_scr, *, tq):
    del mix_ref
    tp = q_ref.shape[0]
    ri = lax.broadcasted_iota(jnp.int32, (tq, tq), 0) // CHUNK
    ci = lax.broadcasted_iota(jnp.int32, (tq, tq), 1) // CHUNK
    diag_visible = ci <= ri

    for qi in range(tp // tq):
        q = q_ref[qi * tq:(qi + 1) * tq, :]

        def scores(kb, macc):
            start = pl.multiple_of(kb * tq, tq)
            s = _dot_nt(q, k_ref[pl.ds(start, tq), :])
            s_scr[kb] = s
            return jnp.maximum(macc, _lane_fold(s, jnp.maximum))

        macc = lax.fori_loop(0, qi, scores, jnp.full((tq, LANE), NEG_BIG, F32))
        s = _dot_nt(q, k_ref[qi * tq:(qi + 1) * tq, :])
        s = jnp.where(diag_visible, s, NEG_BIG)
        s_scr[qi] = s
        macc = jnp.maximum(macc, _lane_fold(s, jnp.maximum))
        m = jnp.max(macc, axis=-1, keepdims=True)

        def weights(kb, acc):
            start = pl.multiple_of(kb * tq, tq)
            p = jnp.exp2(s_scr[kb] - m)
            return acc + _dot(p.astype(BF16), v_ref[pl.ds(start, tq), :])

        acc = lax.fori_loop(0, qi + 1, weights, jnp.zeros((tq, 2 * V_DIM), F32))
        o_ref[qi * tq:(qi + 1) * tq, :] = (acc[:, :V_DIM] / acc[:, V_DIM:]).astype(o_ref.dtype)


def _attn_prompt(q, k, v, mix, *, bp, tp):
    tq = min(512, tp)
    r, mw = mix.shape
    col0 = (mw // 2) // V_DIM
    return pl.pallas_call(
        functools.partial(_attn_prompt_kernel, tq=tq),
        out_shape=jax.ShapeDtypeStruct((r, mw), mix.dtype),
        grid=(bp, MLA_HEADS),
        in_specs=[
            pl.BlockSpec((None, tp, 2 * LANE), lambda b, h: (h, b, 0)),
            pl.BlockSpec((None, tp, 2 * LANE), lambda b, h: (h, b, 0)),
            pl.BlockSpec((None, tp, 2 * V_DIM), lambda b, h: (h, b, 0)),
            pl.BlockSpec(memory_space=pl.ANY),
        ],
        out_specs=pl.BlockSpec((tp, V_DIM), lambda b, h: (b, col0 + h)),
        scratch_shapes=[pltpu.VMEM((tp // tq, tq, tq), F32)],
        input_output_aliases={3: 0},
        compiler_params=_cparams(("arbitrary", "arbitrary")),
        name="attn_prompt",
    )(q, k, v, mix)


def _attn_sample_kernel(q_ref, kp_ref, vp_ref, kn_ref, vn_ref, mix_ref, o_ref, *, past):
    del mix_ref
    q = q_ref[...]
    ts = q.shape[0]
    sp = _dot_nt(q, kp_ref[...])
    sn = _dot_nt(q, kn_ref[...])
    row = past + lax.broadcasted_iota(jnp.int32, (ts, ts), 0)
    col = past + lax.broadcasted_iota(jnp.int32, (ts, ts), 1)
    sn = jnp.where(col < (row // CHUNK + 1) * CHUNK, sn, NEG_BIG)
    m = jnp.maximum(jnp.max(sp, axis=-1, keepdims=True), jnp.max(sn, axis=-1, keepdims=True))
    pp = jnp.exp2(sp - m)
    pn = jnp.exp2(sn - m)
    o = _dot(pp.astype(BF16), vp_ref[...]) + _dot(pn.astype(BF16), vn_ref[...])
    o_ref[...] = (o[:, :V_DIM] / o[:, V_DIM:]).astype(o_ref.dtype)


def _attn_sample(q, k_past, v_past, k_new, v_new, mix, *, bs, ts, past, row0):
    r, mw = mix.shape
    col0 = (mw // 2) // V_DIM
    g0 = row0 // ts
    return pl.pallas_call(
        functools.partial(_attn_sample_kernel, past=past),
        out_shape=jax.ShapeDtypeStruct((r, mw), mix.dtype),
        grid=(bs, MLA_HEADS),
        in_specs=[
            pl.BlockSpec((None, ts, 2 * LANE), lambda b, h: (h, g0 + b, 0)),
            pl.BlockSpec((None, past, 2 * LANE), lambda b, h: (h, b, 0)),
            pl.BlockSpec((None, past, 2 * V_DIM), lambda b, h: (h, b, 0)),
            pl.BlockSpec((None, ts, 2 * LANE), lambda b, h: (h, g0 + b, 0)),
            pl.BlockSpec((None, ts, 2 * V_DIM), lambda b, h: (h, g0 + b, 0)),
            pl.BlockSpec(memory_space=pl.ANY),
        ],
        out_specs=pl.BlockSpec((ts, V_DIM), lambda b, h: (g0 + b, col0 + h)),
        input_output_aliases={5: 0},
        compiler_params=_cparams(("arbitrary", "arbitrary")),
        name="attn_sample",
    )(q, k_past, v_past, k_new, v_new, mix)


def _ret_kernel(*refs, has_init, has_alias):
    refs = list(refs)
    q_ref, k_ref, v_ref, g_ref = refs[:4]
    del refs[:4]
    if has_init:
        s0_ref = refs.pop(0)
    dm_ref, qd_ref, kd_ref, cd_ref = refs[:4]
    del refs[:4]
    if has_alias:
        del refs[:1]
    o_ref, s_ref = refs
    hpg, kdim, vdim = s_ref.shape

    @pl.when(pl.program_id(2) == 0)
    def _():
        if has_init:
            s_ref[...] = s0_ref[...]
        else:
            s_ref[...] = jnp.zeros_like(s_ref)

    for h in range(hpg):
        q = q_ref[:, h * kdim:(h + 1) * kdim]
        k = k_ref[:, h * kdim:(h + 1) * kdim]
        v = v_ref[:, h * vdim:(h + 1) * vdim]
        s = s_ref[h]
        att = _dot_nt(q, k) * dm_ref[h]
        o = _dot(att.astype(BF16), v) + _dot(q, s.astype(BF16)) * qd_ref[:, h:h + 1]
        kdec = (k.astype(F32) * kd_ref[:, h:h + 1]).astype(BF16)
        s_ref[h] = s * cd_ref[:, h:h + 1] + _dot_tn(kdec, v)
        mu = jnp.mean(o, axis=-1, keepdims=True)
        oc = o - mu
        var = jnp.mean(oc * oc, axis=-1, keepdims=True)
        g = g_ref[:, h * vdim:(h + 1) * vdim].astype(F32)
        o_ref[:, h * vdim:(h + 1) * vdim] = (_silu(g) * (oc * lax.rsqrt(var + EPS))).astype(o_ref.dtype)


def _retention_tables(heads, hpg, L):
    lg = jnp.log1p(-jnp.exp2(-5.0 - jnp.arange(heads, dtype=F32)))
    idx = jnp.arange(L, dtype=F32)
    rel = idx[:, None] - idx[None, :]
    dmask = jnp.exp(jnp.where(rel[None] >= 0, rel[None] * lg[:, None, None], -jnp.inf))
    grp = lambda a: a.reshape(a.shape[0], heads // hpg, hpg).transpose(1, 0, 2)
    qdec = grp(jnp.exp((idx[:, None] + 1.0) * lg[None, :]))
    kdec = grp(jnp.exp((L - 1.0 - idx[:, None]) * lg[None, :]))
    cdec = grp(jnp.exp(L * lg)[None, :])
    return dmask, qdec, kdec, cdec


def _retention(qk, vg, *, L, nseq, ncs, row0, heads, s0=None, o_prev=None):
    r = qk.shape[0]
    kdim = qk.shape[1] // (2 * heads)
    vdim = vg.shape[1] // (2 * heads)
    hpg = 4
    nhg = heads // hpg
    dmask, qdec, kdec, cdec = _retention_tables(heads, hpg, L)
    rb0 = row0 // L
    rowblk = lambda s, c: rb0 + s * ncs + c
    in_specs = [
        pl.BlockSpec((L, hpg * kdim), lambda hg, s, c: (rowblk(s, c), hg)),
        pl.BlockSpec((L, hpg * kdim), lambda hg, s, c: (rowblk(s, c), nhg + hg)),
        pl.BlockSpec((L, hpg * vdim), lambda hg, s, c: (rowblk(s, c), hg)),
        pl.BlockSpec((L, hpg * vdim), lambda hg, s, c: (rowblk(s, c), nhg + hg)),
    ]
    args = [qk, qk, vg, vg]
    if s0 is not None:
        in_specs.append(pl.BlockSpec((None, hpg, kdim, vdim), lambda hg, s, c: (s, hg, 0, 0)))
        args.append(s0)
    in_specs += [pl.BlockSpec((hpg, L, L), lambda hg, s, c: (hg, 0, 0)),
                 pl.BlockSpec((None, L, hpg), lambda hg, s, c: (hg, 0, 0)),
                 pl.BlockSpec((None, L, hpg), lambda hg, s, c: (hg, 0, 0)),
                 pl.BlockSpec((None, 1, hpg), lambda hg, s, c: (hg, 0, 0))]
    args += [dmask, qdec, kdec, cdec]
    aliases = {}
    if o_prev is not None:
        in_specs.append(pl.BlockSpec(memory_space=pl.ANY))
        aliases = {len(args): 0}
        args.append(o_prev)
    return pl.pallas_call(
        functools.partial(_ret_kernel, has_init=s0 is not None, has_alias=o_prev is not None),
        out_shape=(jax.ShapeDtypeStruct((r, heads * vdim), BF16),
                   jax.ShapeDtypeStruct((nseq, heads, kdim, vdim), F32)),
        grid=(nhg, nseq, ncs),
        in_specs=in_specs,
        out_specs=(pl.BlockSpec((L, hpg * vdim), lambda hg, s, c: (rowblk(s, c), hg)),
                   pl.BlockSpec((None, hpg, kdim, vdim), lambda hg, s, c: (s, hg, 0, 0))),
        input_output_aliases=aliases,
        compiler_params=_cparams(("arbitrary", "arbitrary", "arbitrary")),
        name="retention",
    )(*args)


def _pad_rope_cols(a):
    half = QK_ROPE // 2
    z = jnp.zeros(a.shape[:-1] + (LANE // 2 - half,), a.dtype)
    return jnp.concatenate([a[..., :half], z, a[..., half:], z], axis=-1)


def _unpad_rope_cols(a):
    half = QK_ROPE // 2
    return jnp.concatenate([a[..., :half], a[..., LANE // 2:LANE // 2 + half]], axis=-1)


def _rope_tables(pos, half):
    inv = ROPE_THETA ** (-jnp.arange(half, dtype=F32) / half)
    ang = pos.astype(F32)[:, None] * inv[None, :]
    return jnp.cos(ang), jnp.sin(ang)


@jax.jit
def kernel(x_prompt, x_sample, c_prompt, c_sample, cache_mla_ckv, cache_mla_kpe, state_ssd, state_ssd_conv, state_ret, norm_mix, norm_ffn, w_ada, b_ada, w_in0, conv_w, conv_b, dt_bias, a_log, d_skip, ssd_norm, q_a_norm, w_uq, q_norm_nope, q_norm_rope, kv_a_norm, w_ukv, k_norm_nope, k_norm_rope, w_out0, w_in1, w_out1, w_gate, w_up, w_down):
    bp, tp, d = x_prompt.shape
    bs, ts, _ = x_sample.shape
    past = cache_mla_ckv.shape[2]
    assert ts == CHUNK and tp % CHUNK == 0 and past % CHUNK == 0
    rp, rs = bp * tp, bs * ts
    r = rp + rs
    ncp = tp // CHUNK
    bm = min(1024, r)
    assert r % bm == 0 and bm % 512 == 0

    ssd_width = d // 2
    ssd_heads = ssd_width // SSD_HEADDIM
    conv_ch = ssd_width + 2 * SSD_GROUPS * SSD_STATE
    off_dt = ssd_width + conv_ch
    off_cq = off_dt + ssd_heads
    off_ckv = off_cq + Q_LORA
    off_kpe = off_ckv + KV_LORA
    assert conv_ch == 2 * ssd_width and ssd_heads <= LANE
    mix0 = ssd_width + MLA_HEADS * V_DIM
    ret_kdim = d // RET_HEADS
    ret_qk = RET_HEADS * ret_kdim

    wi = w_in0[0]
    w0_main = jnp.concatenate([wi[:, :off_dt], wi[:, off_cq:off_kpe]], axis=1).astype(BF16)
    w0_small = jnp.concatenate([wi[:, off_dt:off_cq], jnp.zeros((d, LANE - ssd_heads), F32),
                                _pad_rope_cols(wi[:, off_kpe:])], axis=1).astype(BF16)
    wq = w_uq[0].reshape(Q_LORA, MLA_HEADS, QK_NOPE + QK_ROPE)
    w_uq_p = jnp.concatenate([wq[:, :, :QK_NOPE].reshape(Q_LORA, -1),
                              _pad_rope_cols(wq[:, :, QK_NOPE:]).reshape(Q_LORA, -1)], axis=1).astype(BF16)
    wkv = w_ukv[0].reshape(KV_LORA, MLA_HEADS, QK_NOPE + V_DIM)
    w_ukv_p = jnp.concatenate([wkv[:, :, :QK_NOPE].reshape(KV_LORA, -1),
                               wkv[:, :, QK_NOPE:].reshape(KV_LORA, -1)], axis=1).astype(BF16)
    w_out0_b = w_out0[0].astype(BF16)
    w1_b = w_in1[0].astype(BF16)
    w_out1_b = w_out1[0].astype(BF16)
    w_gate_b, w_up_b, w_down_b = w_gate.astype(BF16), w_up.astype(BF16), w_down.astype(BF16)

    def lane_pad(a, n=LANE):
        return jnp.pad(a, (0, n - a.shape[0])).reshape(1, n)

    pos = jnp.concatenate([jnp.tile(jnp.arange(tp), bp), jnp.tile(past + jnp.arange(ts), bs)])
    c32, s32 = _rope_tables(pos, QK_ROPE // 2)
    zq = jnp.zeros_like(c32)
    cos_m = jnp.concatenate([c32, zq, c32, zq], axis=1)
    sin_m = jnp.concatenate([-s32, zq, s32, zq], axis=1)
    cos_r, sin_r = _rope_tables(pos, ret_kdim // 2)
    lp = min(256, tp)

    nb = -(-(bp + bs) // 8) * 8
    c_all = jnp.concatenate([c_prompt, c_sample, jnp.zeros((nb - bp - bs, d), F32)], axis=0)
    mod = _ada(c_all, w_ada, b_ada)
    depth = mod.shape[0]
    mod_p = jnp.broadcast_to(mod[:, :bp, None, :], (depth, bp, ncp, 6 * d)).reshape(depth, bp * ncp, 6 * d)
    tbl = jnp.concatenate([mod_p, mod[:, bp:bp + bs]], axis=1).reshape(depth, r // CHUNK, 6, d)

    x_groups = ((x_prompt.reshape(rp, d), 0), (x_sample.reshape(rs, d), rp))
    bm_res = bm if rp % bm == 0 and rs % bm == 0 else 512
    assert rp % bm_res == 0 and rs % bm_res == 0
    bm_wide = r // 8
    assert bm_wide % 16 == 0

    def ffn(x, i, split_out=False):
        hn = _normmod(x, norm_ffn[i], tbl[i, :, 4], tbl[i, :, 3])
        hid = _mm(hn, [w_gate_b, w_up_b], layer=i, bm=bm_wide, bn=256, out_dtype=BF16, epi=_epi_swiglu,
                  name="ffn_gate_up")
        down = functools.partial(_residual_mm, hid, w_down_b, x, tbl[i, :, 5], layer=i, bm=512, bn=512,
                                 name="ffn_down")
        if not split_out:
            return down()
        return [down(rows=n, x_row0=row0, res_row0=row0) for row0, n in ((0, rp), (rp, rs))]

    hn = None
    for x_part, row0 in x_groups:
        hn = _normmod(x_part, norm_mix[0], tbl[0, :, 1], tbl[0, :, 0], out_rows=r, row0=row0, prev=hn)
    proj = _mm(hn, [w0_main], bm=bm, bn=768, out_dtype=F32, epi=_epi_plain, name="in_proj0")
    small = _mm(hn, [w0_small], bm=bm, bn=2 * LANE, out_dtype=F32, epi=_epi_plain, name="in_proj0_small")

    ssd_params = (conv_w[0], conv_b[0].reshape(1, -1), lane_pad(dt_bias[0]), lane_pad(a_log[0]),
                  lane_pad(d_skip[0]), ssd_norm[0].reshape(1, -1))
    mix, h_p, conv_p8 = _ssd(proj, small, ssd_params, L=lp, nseq=bp, ncs=tp // lp, row0=0, mix_width=mix0)
    mix, h_s, conv_s8 = _ssd(proj, small, ssd_params, L=ts, nseq=bs, ncs=1, row0=rp, mix_width=mix0,
                             init=(state_ssd_conv[0], state_ssd[0].reshape(bs, ssd_width, SSD_STATE)), mix=mix)

    gr_q = _pad_rope_cols(q_norm_rope[0]).reshape(1, LANE)
    gr_k = _pad_rope_cols(k_norm_rope[0]).reshape(1, LANE)
    cq_block = (off_dt + 0) // Q_LORA
    ckv_block = (off_dt + Q_LORA) // KV_LORA
    q_all = _mla_q(proj, cq_block, q_a_norm[0].reshape(1, -1), w_uq_p, q_norm_nope[0].reshape(1, -1), gr_q,
                   cos_m, sin_m)
    ckv_all, kpe_all = _mla_ckv(proj, small, ckv_block, kv_a_norm[0].reshape(1, -1), gr_k, cos_m, sin_m)
    gk = k_norm_nope[0].reshape(1, -1)
    k_all, v_all = _mla_kvup(ckv_all, kpe_all, w_ukv_p, gk)
    k_past, v_past = _mla_kvup(cache_mla_ckv[0].reshape(bs * past, KV_LORA),
                               _pad_rope_cols(cache_mla_kpe[0].reshape(bs * past, QK_ROPE)), w_ukv_p, gk)
    mix = _attn_prompt(q_all, k_all, v_all, mix, bp=bp, tp=tp)
    mix = _attn_sample(q_all, k_past, v_past, k_all, v_all, mix, bs=bs, ts=ts, past=past, row0=rp)

    x = None
    for x_part, row0 in x_groups:
        x = _residual_mm(mix, w_out0_b, x_part, tbl[0, :, 2], bm=bm_res, bn=512, name="out_proj0",
                         rows=x_part.shape[0], x_row0=row0, out_rows=r, out_row0=row0, prev=x)
    x = ffn(x, 0)

    hn = _normmod(x, norm_mix[1], tbl[1, :, 1], tbl[1, :, 0])
    hpb = 1024 // ret_kdim
    rope_epi = functools.partial(_epi_rope_qk, head_dim=ret_kdim, k_block0=RET_HEADS // hpb,
                                 k_scale=ret_kdim ** -0.5)
    half = ret_kdim // 2
    qk = _mm(hn, [w1_b], n=2 * ret_qk, bm=bm, bn=1024, out_dtype=BF16, epi=rope_epi,
             extras=[(cos_r, (bm, half), lambda i, j: (i, 0)), (sin_r, (bm, half), lambda i, j: (i, 0))],
             name="in_proj1_qk")
    vg = _mm(hn, [w1_b], n=w1_b.shape[1] - 2 * ret_qk, w_col0=2 * ret_qk, bm=bm, bn=1024, out_dtype=BF16,
             epi=_epi_plain, name="in_proj1_vg")
    o_ret, ret_p = _retention(qk, vg, L=lp, nseq=bp, ncs=tp // lp, row0=0, heads=RET_HEADS)
    o_ret, ret_s = _retention(qk, vg, L=ts, nseq=bs, ncs=1, row0=rp, heads=RET_HEADS, s0=state_ret[0],
                              o_prev=o_ret)
    x = _residual_mm(o_ret, w_out1_b, x, tbl[1, :, 2], bm=bm, bn=1024, bk=2048, name="out_proj1")
    y_p, y_s = ffn(x, 1, split_out=True)

    def split(a, shape_p, shape_s, n=rp):
        return a[:n].reshape(shape_p), a[n:].reshape(shape_s)

    y_p, y_s = y_p.reshape(bp, tp, d), y_s.reshape(bs, ts, d)
    ckv_p, ckv_s = split(ckv_all, (1, bp, tp, KV_LORA), (1, bs, ts, KV_LORA))
    kpe_p, kpe_s = split(_unpad_rope_cols(kpe_all), (1, bp, tp, QK_ROPE), (1, bs, ts, QK_ROPE))
    hshape = (ssd_heads, SSD_HEADDIM, SSD_STATE)
    ssd_p, ssd_s = h_p.reshape((1, bp) + hshape), h_s.reshape((1, bs) + hshape)
    conv_p, conv_s = conv_p8[None, :, 8 - (CONV_W - 1):], conv_s8[None, :, 8 - (CONV_W - 1):]
    return (y_p, y_s, ckv_p, kpe_p, ssd_p, conv_p, ret_p[None], ckv_s, kpe_s, ssd_s, conv_s, ret_s[None])
```

```python
import functools
import math

import jax
import jax.numpy as jnp
from jax import lax
from jax.experimental import pallas as pl
from jax.experimental.pallas import tpu as pltpu

F32 = jnp.float32
BF16 = jnp.bfloat16

CHUNK = 64
EPS = 1e-6
NEG_BIG = -1e30

SSD_HEADDIM = 64
SSD_GROUPS = 8
SSD_STATE = 128
CONV_W = 4
MLA_HEADS = 16
QK_NOPE = 128
QK_ROPE = 64
V_DIM = 128
Q_LORA = 1024
KV_LORA = 512
ROPE_THETA = 10000.0
MLA_SCALE = (QK_NOPE + QK_ROPE) ** -0.5
Q_PRESCALE = MLA_SCALE * math.log2(math.e)
RET_HEADS = 16
LANE = 128

VMEM_LIMIT = 56 * 1024 * 1024


def _cparams(sem):
    return pltpu.CompilerParams(dimension_semantics=sem, vmem_limit_bytes=VMEM_LIMIT)


def _sigmoid(x):
    return 1.0 / (1.0 + jnp.exp(-x))


def _silu(x):
    return x * _sigmoid(x)


def _softplus(x):
    return jnp.maximum(x, 0.0) + jnp.log1p(jnp.exp(-jnp.abs(x)))


def _dot(a, b):
    return jnp.dot(a, b, preferred_element_type=F32)


def _dot_nt(a, b):
    return lax.dot_general(a, b, (((1,), (1,)), ((), ())), preferred_element_type=F32)


def _dot_tn(a, b):
    return lax.dot_general(a, b, (((0,), (0,)), ((), ())), preferred_element_type=F32)


def _ada_kernel(c_ref, w_ref, b_ref, o_ref):
    c = c_ref[...]
    o_ref[...] = _dot(_silu(c).astype(BF16), w_ref[...].astype(BF16)) + b_ref[...]


def _ada(c_all, w_ada, b_ada):
    depth, d, n = w_ada.shape
    nb = c_all.shape[0]
    tn = 512
    return pl.pallas_call(
        _ada_kernel,
        out_shape=jax.ShapeDtypeStruct((depth, nb, n), F32),
        grid=(depth, n // tn),
        in_specs=[
            pl.BlockSpec((nb, d), lambda l, j: (0, 0)),
            pl.BlockSpec((None, d, tn), lambda l, j: (l, 0, j)),
            pl.BlockSpec((None, 1, tn), lambda l, j: (l, 0, j)),
        ],
        out_specs=pl.BlockSpec((None, nb, tn), lambda l, j: (l, 0, j)),
        compiler_params=_cparams(("arbitrary", "arbitrary")),
        name="ada",
    )(c_all, w_ada, b_ada.reshape(depth, 1, n))


def _normmod_kernel(x_ref, g_ref, sc_ref, sh_ref, *rest):
    o_ref = rest[-1]
    x = x_ref[...]
    rb, d = x.shape
    y = x * lax.rsqrt(jnp.mean(x * x, axis=-1, keepdims=True) + EPS) * g_ref[...]
    y = y.reshape(rb // CHUNK, CHUNK, d)
    y = y * (1.0 + sc_ref[...][:, None, :]) + sh_ref[...][:, None, :]
    o_ref[...] = y.reshape(rb, d).astype(o_ref.dtype)


def _normmod(x, gain, sc_tbl, sh_tbl, *, out_rows=None, row0=0, prev=None):
    r, d = x.shape
    rb = 512
    gb = rb // CHUNK
    b0 = row0 // rb
    out_rows = r if out_rows is None else out_rows
    in_specs = [
        pl.BlockSpec((rb, d), lambda i: (i, 0)),
        pl.BlockSpec((1, d), lambda i: (0, 0)),
        pl.BlockSpec((gb, d), lambda i: (b0 + i, 0)),
        pl.BlockSpec((gb, d), lambda i: (b0 + i, 0)),
    ]
    args = [x, gain.reshape(1, d), sc_tbl, sh_tbl]
    aliases = {}
    if prev is not None:
        in_specs.append(pl.BlockSpec(memory_space=pl.ANY))
        args.append(prev)
        aliases = {4: 0}
    return pl.pallas_call(
        _normmod_kernel,
        out_shape=jax.ShapeDtypeStruct((out_rows, d), BF16),
        grid=(r // rb,),
        in_specs=in_specs,
        out_specs=pl.BlockSpec((rb, d), lambda i: (b0 + i, 0)),
        input_output_aliases=aliases,
        compiler_params=_cparams(("arbitrary",)),
        name="normmod",
    )(*args)


def _mm_kernel(*refs, n_w, n_ex, nk, epi, has_alias):
    x_ref = refs[0]
    w_refs = refs[1:1 + n_w]
    ex_refs = refs[1 + n_w:1 + n_w + n_ex]
    refs = refs[1 + n_w + n_ex + (1 if has_alias else 0):]
    o_ref = refs[0]
    acc_refs = refs[1:]
    j = pl.program_id(1)
    if nk == 1:
        accs = [_dot(x_ref[...], w[...]) for w in w_refs]
        o_ref[...] = epi(accs, ex_refs, j).astype(o_ref.dtype)
    else:
        k = pl.program_id(2)

        @pl.when(k == 0)
        def _():
            for a in acc_refs:
                a[...] = jnp.zeros_like(a)

        for a, w in zip(acc_refs, w_refs):
            a[...] += _dot(x_ref[...], w[...])

        @pl.when(k == nk - 1)
        def _():
            o_ref[...] = epi([a[...] for a in acc_refs], ex_refs, j).astype(o_ref.dtype)


def _mm(x, ws, *, bm, bn, bk=None, out_dtype, epi, extras=(), name, n=None, w_col0=0, layer=None, rows=None,
        x_row0=0, out_rows=None, out_row0=0, prev=None):
    kdim = x.shape[1]
    rows = x.shape[0] if rows is None else rows
    n = ws[0].shape[-1] if n is None else n
    out_rows = rows if out_rows is None else out_rows
    bk = kdim if bk is None else bk
    nk = kdim // bk
    xb, ob, wb = x_row0 // bm, out_row0 // bm, w_col0 // bn
    in_specs = [pl.BlockSpec((bm, bk), lambda i, j, k: (xb + i, k))]
    if layer is None:
        in_specs += [pl.BlockSpec((bk, bn), lambda i, j, k: (k, wb + j)) for _ in ws]
    else:
        in_specs += [pl.BlockSpec((None, bk, bn), lambda i, j, k: (layer, k, wb + j)) for _ in ws]
    for _, bs, im in extras:
        in_specs.append(pl.BlockSpec(bs, functools.partial(lambda i, j, k, im: im(i, j), im=im)))
    args = [x, *ws, *[e[0] for e in extras]]
    aliases = {}
    if prev is not None:
        in_specs.append(pl.BlockSpec(memory_space=pl.ANY))
        aliases = {len(args): 0}
        args.append(prev)
    scratch = [pltpu.VMEM((bm, bn), F32) for _ in ws] if nk > 1 else []
    return pl.pallas_call(
        functools.partial(_mm_kernel, n_w=len(ws), n_ex=len(extras), nk=nk, epi=epi, has_alias=prev is not None),
        out_shape=jax.ShapeDtypeStruct((out_rows, n), out_dtype),
        grid=(rows // bm, n // bn, nk),
        in_specs=in_specs,
        out_specs=pl.BlockSpec((bm, bn), lambda i, j, k: (ob + i, j)),
        scratch_shapes=scratch,
        input_output_aliases=aliases,
        compiler_params=_cparams(("arbitrary", "arbitrary", "arbitrary")),
        name=name,
    )(*args)


def _epi_plain(accs, ex, j):
    return accs[0]


def _epi_swiglu(accs, ex, j):
    return _silu(accs[0]) * accs[1]


def _epi_residual(accs, ex, j):
    res_ref, gate_ref = ex
    acc = accs[0]
    bm, bn = acc.shape
    upd = acc.reshape(bm // CHUNK, CHUNK, bn) * gate_ref[...][:, None, :]
    return res_ref[...] + upd.reshape(bm, bn)


def _epi_rope_qk(accs, ex, j, *, head_dim, k_block0, k_scale):
    cos_ref, sin_ref = ex
    acc = accs[0]
    c, s = cos_ref[...], sin_ref[...]
    half = head_dim // 2
    scale = jnp.where(j >= k_block0, k_scale, 1.0).astype(F32)
    outs = []
    for h in range(acc.shape[1] // head_dim):
        x1 = acc[:, h * head_dim:h * head_dim + half]
        x2 = acc[:, h * head_dim + half:(h + 1) * head_dim]
        outs.append((x1 * c - x2 * s) * scale)
        outs.append((x1 * s + x2 * c) * scale)
    return jnp.concatenate(outs, axis=-1)


def _residual_mm(x, w, res, gate_tbl, *, bm, bn, bk=None, name, layer=None, rows=None, x_row0=0, res_row0=0,
                 out_rows=None, out_row0=0, prev=None):
    gb = bm // CHUNK
    xb, rb = x_row0 // bm, res_row0 // bm
    return _mm(x, [w], bm=bm, bn=bn, bk=bk, out_dtype=F32, epi=_epi_residual,
               extras=[(res, (bm, bn), lambda i, j: (rb + i, j)), (gate_tbl, (gb, bn), lambda i, j: (xb + i, j))],
               name=name, layer=layer, rows=rows, x_row0=x_row0, out_rows=out_rows, out_row0=out_row0, prev=prev)


def _ssd_kernel(*refs, has_init, has_alias):
    refs = list(refs)
    z_ref, xa_ref, xb_ref, sm_ref = refs[:4]
    del refs[:4]
    if has_init:
        cinit_ref, h0_ref = refs[:2]
        del refs[:2]
    cw_ref, cb_ref, dtb_ref, alog_ref, dsk_ref, nrm_ref = refs[:6]
    del refs[:6]
    if has_alias:
        del refs[:1]
    y_ref, hout_ref, cout_ref, ext, xc, ysc = refs
    L, width = xa_ref.shape
    heads = width // SSD_HEADDIM
    rpg = heads // SSD_GROUPS
    gw = rpg * SSD_HEADDIM

    @pl.when(pl.program_id(1) == 0)
    def _():
        if has_init:
            ext[5:8, :] = cinit_ref[...]
            hout_ref[...] = h0_ref[...]
        else:
            ext[0:8, :] = jnp.zeros((8, ext.shape[1]), F32)
            hout_ref[...] = jnp.zeros_like(hout_ref)

    ext[8:8 + L, 0:width] = xa_ref[...]
    ext[8:8 + L, width:2 * width] = xb_ref[...]

    cstep = 512
    for c in range(2 * width // cstep):
        sl = slice(c * cstep, (c + 1) * cstep)
        a = cb_ref[:, sl] + cw_ref[0:1, sl] * ext[5:5 + L, sl]
        for w in range(1, CONV_W):
            a = a + cw_ref[w:w + 1, sl] * ext[5 + w:5 + w + L, sl]
        xc[:, sl] = _silu(a)
    tail_rows = ext[L:L + 8, :]
    cout_ref[...] = tail_rows
    ext[0:8, :] = tail_rows

    dt = _softplus(sm_ref[:, 0:LANE] + dtb_ref[...])
    a_neg = -jnp.exp(alog_ref[...])
    cs = dt * a_neg
    row = lax.broadcasted_iota(jnp.int32, (L, LANE), 0)
    sh = 1
    while sh < L:
        cs = cs + jnp.where(row >= sh, pltpu.roll(cs, sh, axis=0), 0.0)
        sh *= 2
    both_t = jnp.concatenate([cs, dt], axis=0).T
    cs_t = both_t[:, 0:L]
    dt_t = both_t[:, L:2 * L]
    tail_t = jnp.exp(cs_t[:, L - 1:L] - cs_t) * dt_t
    ecs = jnp.exp(cs)

    ii = lax.broadcasted_iota(jnp.int32, (L, L), 0)
    jj = lax.broadcasted_iota(jnp.int32, (L, L), 1)
    causal = ii >= jj

    half = width // 2
    x_t = jnp.concatenate([xc[:, 0:half], xc[:, half:width]], axis=0).T

    for g in range(SSD_GROUPS):
        bg = xc[:, width + g * SSD_STATE:width + (g + 1) * SSD_STATE].astype(BF16)
        cg = xc[:, width + SSD_GROUPS * SSD_STATE + g * SSD_STATE:
                width + SSD_GROUPS * SSD_STATE + (g + 1) * SSD_STATE].astype(BF16)
        cbm = _dot_nt(cg, bg)
        hg = hout_ref[g * gw:(g + 1) * gw, :]
        ystate = _dot_nt(cg, hg.astype(BF16))
        lo = (g * gw) % half
        lanes = slice(0, L) if g * gw < half else slice(L, 2 * L)
        xs_rows, dec_rows = [], []
        for r in range(rpg):
            h = g * rpg + r
            cs_col = cs[:, h:h + 1]
            seg = cs_col - cs_t[h:h + 1, :]
            dec = jnp.exp(jnp.where(causal, seg, NEG_BIG))
            wts = cbm * dec * dt_t[h:h + 1, :]
            xh = xc[:, h * SSD_HEADDIM:(h + 1) * SSD_HEADDIM]
            yh = _dot(wts.astype(BF16), xh.astype(BF16))
            yh = yh + ystate[:, r * SSD_HEADDIM:(r + 1) * SSD_HEADDIM] * ecs[:, h:h + 1]
            yh = yh + xh * dsk_ref[:, h:h + 1]
            ysc[:, h * SSD_HEADDIM:(h + 1) * SSD_HEADDIM] = yh
            xs_rows.append(x_t[lo + r * SSD_HEADDIM:lo + (r + 1) * SSD_HEADDIM, lanes] * tail_t[h:h + 1, :])
            dec_rows.append(jnp.broadcast_to(jnp.exp(cs_t[h:h + 1, L - 1:L]), (SSD_HEADDIM, SSD_STATE)))
        upd = _dot(jnp.concatenate(xs_rows, axis=0).astype(BF16), bg)
        hout_ref[g * gw:(g + 1) * gw, :] = hg * jnp.concatenate(dec_rows, axis=0) + upd

    y = ysc[...] * _silu(z_ref[...])
    outs = []
    for g in range(SSD_GROUPS):
        yg = y[:, g * gw:(g + 1) * gw]
        ms = jnp.mean(yg * yg, axis=-1, keepdims=True)
        outs.append(yg * lax.rsqrt(ms + EPS) * nrm_ref[:, g * gw:(g + 1) * gw])
    y_ref[...] = jnp.concatenate(outs, axis=-1).astype(y_ref.dtype)


def _ssd(proj, small, params, *, L, nseq, ncs, row0, mix_width, init=None, mix=None):
    conv_w, conv_b, dtb, alog, dsk, nrm = params
    r = proj.shape[0]
    width = nrm.shape[1]
    rb0 = row0 // L
    rowblk = lambda s, c: rb0 + s * ncs + c
    const = lambda s, c: (0, 0)
    in_specs = [
        pl.BlockSpec((L, width), lambda s, c: (rowblk(s, c), 0)),
        pl.BlockSpec((L, width), lambda s, c: (rowblk(s, c), 1)),
        pl.BlockSpec((L, width), lambda s, c: (rowblk(s, c), 2)),
        pl.BlockSpec((L, 2 * LANE), lambda s, c: (rowblk(s, c), 0)),
    ]
    args = [proj, proj, proj, small]
    if init is not None:
        in_specs += [pl.BlockSpec((None, CONV_W - 1, 2 * width), lambda s, c: (s, 0, 0)),
                     pl.BlockSpec((None, width, SSD_STATE), lambda s, c: (s, 0, 0))]
        args += list(init)
    in_specs += [pl.BlockSpec((CONV_W, 2 * width), const), pl.BlockSpec((1, 2 * width), const),
                 pl.BlockSpec((1, LANE), const), pl.BlockSpec((1, LANE), const), pl.BlockSpec((1, LANE), const),
                 pl.BlockSpec((1, width), const)]
    args += [conv_w, conv_b, dtb, alog, dsk, nrm]
    aliases = {}
    if mix is not None:
        in_specs.append(pl.BlockSpec(memory_space=pl.ANY))
        aliases = {len(args): 0}
        args.append(mix)
    return pl.pallas_call(
        functools.partial(_ssd_kernel, has_init=init is not None, has_alias=mix is not None),
        out_shape=(jax.ShapeDtypeStruct((r, mix_width), BF16),
                   jax.ShapeDtypeStruct((nseq, width, SSD_STATE), F32),
                   jax.ShapeDtypeStruct((nseq, 8, 2 * width), F32)),
        grid=(nseq, ncs),
        in_specs=in_specs,
        out_specs=(
            pl.BlockSpec((L, width), lambda s, c: (rowblk(s, c), 0)),
            pl.BlockSpec((None, width, SSD_STATE), lambda s, c: (s, 0, 0)),
            pl.BlockSpec((None, 8, 2 * width), lambda s, c: (s, 0, 0)),
        ),
        scratch_shapes=[pltpu.VMEM((L + 8, 2 * width), F32), pltpu.VMEM((L, 2 * width), F32),
                        pltpu.VMEM((L, width), F32)],
        input_output_aliases=aliases,
        compiler_params=_cparams(("arbitrary", "arbitrary")),
        name="ssd",
    )(*args)


def _rope_pad(x, cos_t, sin_t):
    return x * cos_t + pltpu.roll(x, LANE // 2, axis=1) * sin_t


def _mla_q_kernel(cq_ref, ga_ref, w_ref, gn_ref, gr_ref, cos_ref, sin_ref, q_ref):
    cq = cq_ref[...]
    xn = cq * lax.rsqrt(jnp.mean(cq * cq, axis=-1, keepdims=True) + EPS) * ga_ref[...]
    q = _dot(xn.astype(BF16), w_ref[...])
    c, s = cos_ref[...], sin_ref[...]
    nope_w = MLA_HEADS * QK_NOPE
    for h in range(MLA_HEADS):
        qn = q[:, h * QK_NOPE:(h + 1) * QK_NOPE]
        qn = qn * lax.rsqrt(jnp.mean(qn * qn, axis=-1, keepdims=True) + EPS) * gn_ref[...]
        qp = q[:, nope_w + h * LANE:nope_w + (h + 1) * LANE]
        qp = qp * lax.rsqrt(jnp.sum(qp * qp, axis=-1, keepdims=True) * (1.0 / QK_ROPE) + EPS) * gr_ref[...]
        qp = _rope_pad(qp, c, s)
        q_ref[h] = (jnp.concatenate([qn, qp], axis=-1) * Q_PRESCALE).astype(q_ref.dtype)


def _mla_q(proj, cq_block, ga, w_uq_p, gn, gr_pad, cos_t, sin_t):
    r = proj.shape[0]
    bm = 512
    n = w_uq_p.shape[1]
    return pl.pallas_call(
        _mla_q_kernel,
        out_shape=jax.ShapeDtypeStruct((MLA_HEADS, r, 2 * LANE), BF16),
        grid=(r // bm,),
        in_specs=[
            pl.BlockSpec((bm, Q_LORA), lambda i: (i, cq_block)),
            pl.BlockSpec((1, Q_LORA), lambda i: (0, 0)),
            pl.BlockSpec((Q_LORA, n), lambda i: (0, 0)),
            pl.BlockSpec((1, QK_NOPE), lambda i: (0, 0)),
            pl.BlockSpec((1, LANE), lambda i: (0, 0)),
            pl.BlockSpec((bm, LANE), lambda i: (i, 0)),
            pl.BlockSpec((bm, LANE), lambda i: (i, 0)),
        ],
        out_specs=pl.BlockSpec((MLA_HEADS, bm, 2 * LANE), lambda i: (0, i, 0)),
        compiler_params=_cparams(("arbitrary",)),
        name="mla_q",
    )(proj, ga, w_uq_p, gn, gr_pad, cos_t, sin_t)


def _mla_ckv_kernel(ckv_ref, sm_ref, gkv_ref, gr_ref, cos_ref, sin_ref, ckv_out, kpe_out):
    x = ckv_ref[...]
    ckv_out[...] = x * lax.rsqrt(jnp.mean(x * x, axis=-1, keepdims=True) + EPS) * gkv_ref[...]
    kp = sm_ref[:, LANE:2 * LANE]
    kp = kp * lax.rsqrt(jnp.sum(kp * kp, axis=-1, keepdims=True) * (1.0 / QK_ROPE) + EPS) * gr_ref[...]
    kpe_out[...] = _rope_pad(kp, cos_ref[...], sin_ref[...])


def _mla_ckv(proj, small, ckv_block, gkv, gr_pad, cos_t, sin_t):
    r = proj.shape[0]
    bm = 512
    return pl.pallas_call(
        _mla_ckv_kernel,
        out_shape=(jax.ShapeDtypeStruct((r, KV_LORA), F32), jax.ShapeDtypeStruct((r, LANE), F32)),
        grid=(r // bm,),
        in_specs=[
            pl.BlockSpec((bm, KV_LORA), lambda i: (i, ckv_block)),
            pl.BlockSpec((bm, 2 * LANE), lambda i: (i, 0)),
            pl.BlockSpec((1, KV_LORA), lambda i: (0, 0)),
            pl.BlockSpec((1, LANE), lambda i: (0, 0)),
            pl.BlockSpec((bm, LANE), lambda i: (i, 0)),
            pl.BlockSpec((bm, LANE), lambda i: (i, 0)),
        ],
        out_specs=(pl.BlockSpec((bm, KV_LORA), lambda i: (i, 0)), pl.BlockSpec((bm, LANE), lambda i: (i, 0))),
        compiler_params=_cparams(("arbitrary",)),
        name="mla_ckv",
    )(proj, small, gkv, gr_pad, cos_t, sin_t)


def _mla_kvup_kernel(*refs, compact):
    if compact:
        ckv_ref, w_ref, gk_ref, k_ref, v_ref = refs
    else:
        ckv_ref, kpe_ref, w_ref, gk_ref, k_ref, v_ref = refs
    kv = _dot(ckv_ref[...].astype(BF16), w_ref[...])
    nope_w = MLA_HEADS * QK_NOPE
    for h in range(MLA_HEADS):
        kn = kv[:, h * QK_NOPE:(h + 1) * QK_NOPE]
        kn = kn * lax.rsqrt(jnp.mean(kn * kn, axis=-1, keepdims=True) + EPS) * gk_ref[...]
        vh = kv[:, nope_w + h * V_DIM:nope_w + (h + 1) * V_DIM]
        if not compact:
            kn = jnp.concatenate([kn, kpe_ref[...]], axis=-1)
            vh = jnp.concatenate([vh, jnp.ones_like(vh)], axis=-1)
        k_ref[h] = kn.astype(k_ref.dtype)
        v_ref[h] = vh.astype(v_ref.dtype)


def _mla_kvup(ckv, kpe_pad, w_ukv_p, gk):
    r = ckv.shape[0]
    bm = 512
    n = w_ukv_p.shape[1]
    compact = kpe_pad is None
    kw, vw = (QK_NOPE, V_DIM) if compact else (2 * LANE, 2 * V_DIM)
    in_specs = [pl.BlockSpec((bm, KV_LORA), lambda i: (i, 0))]
    args = [ckv]
    if not compact:
        in_specs.append(pl.BlockSpec((bm, LANE), lambda i: (i, 0)))
        args.append(kpe_pad)
    in_specs += [pl.BlockSpec((KV_LORA, n), lambda i: (0, 0)), pl.BlockSpec((1, QK_NOPE), lambda i: (0, 0))]
    args += [w_ukv_p, gk]
    return pl.pallas_call(
        functools.partial(_mla_kvup_kernel, compact=compact),
        out_shape=(jax.ShapeDtypeStruct((MLA_HEADS, r, kw), BF16),
                   jax.ShapeDtypeStruct((MLA_HEADS, r, vw), BF16)),
        grid=(r // bm,),
        in_specs=in_specs,
        out_specs=(pl.BlockSpec((MLA_HEADS, bm, kw), lambda i: (0, i, 0)),
                   pl.BlockSpec((MLA_HEADS, bm, vw), lambda i: (0, i, 0))),
        compiler_params=_cparams(("arbitrary",)),
        name="mla_kvup",
    )(*args)


def _lane_fold(x, op):
    out = x[:, 0:LANE]
    for c in range(1, x.shape[1] // LANE):
        out = op(out, x[:, c * LANE:(c + 1) * LANE])
    return out


def _attn_prompt_kernel(q_ref, k_ref, v_ref, mix_ref, o_ref, s_scr, *, tq):
    del mix_ref
    tp = q_ref.shape[0]
    ri = lax.broadcasted_iota(jnp.int32, (tq, tq), 0) // CHUNK
    ci = lax.broadcasted_iota(jnp.int32, (tq, tq), 1) // CHUNK
    diag_visible = ci <= ri

    for qi in range(tp // tq):
        q = q_ref[qi * tq:(qi + 1) * tq, :]

        macc = None
        for kb in range(qi + 1):
            s = _dot_nt(q, k_ref[kb * tq:(kb + 1) * tq, :])
            if kb == qi:
                s = jnp.where(diag_visible, s, NEG_BIG)
            s_scr[kb] = s
            fold = _lane_fold(s, jnp.maximum)
            macc = fold if macc is None else jnp.maximum(macc, fold)
        m = jnp.max(macc, axis=-1, keepdims=True)

        acc = None
        for kb in range(qi + 1):
            p = jnp.exp2(s_scr[kb] - m)
            pv = _dot(p.astype(BF16), v_ref[kb * tq:(kb + 1) * tq, :])
            acc = pv if acc is None else acc + pv
        o_ref[qi * tq:(qi + 1) * tq, :] = (acc[:, :V_DIM] / acc[:, V_DIM:]).astype(o_ref.dtype)


def _attn_prompt(q, k, v, mix, *, bp, tp):
    tq = min(512, tp)
    r, mw = mix.shape
    col0 = (mw // 2) // V_DIM
    return pl.pallas_call(
        functools.partial(_attn_prompt_kernel, tq=tq),
        out_shape=jax.ShapeDtypeStruct((r, mw), mix.dtype),
        grid=(bp, MLA_HEADS),
        in_specs=[
            pl.BlockSpec((None, tp, 2 * LANE), lambda b, h: (h, b, 0)),
            pl.BlockSpec((None, tp, 2 * LANE), lambda b, h: (h, b, 0)),
            pl.BlockSpec((None, tp, 2 * V_DIM), lambda b, h: (h, b, 0)),
            pl.BlockSpec(memory_space=pl.ANY),
        ],
        out_specs=pl.BlockSpec((tp, V_DIM), lambda b, h: (b, col0 + h)),
        scratch_shapes=[pltpu.VMEM((tp // tq, tq, tq), F32)],
        input_output_aliases={3: 0},
        compiler_params=_cparams(("arbitrary", "arbitrary")),
        name="attn_prompt",
    )(q, k, v, mix)


def _attn_sample_kernel(q_ref, kp_ref, kpe_ref, vp_ref, kn_ref, vn_ref, mix_ref, o_ref, *, past):
    del mix_ref
    q = q_ref[...]
    ts = q.shape[0]
    k_past = jnp.concatenate([kp_ref[...], kpe_ref[...]], axis=-1)
    sp = _dot_nt(q, k_past)
    sn = _dot_nt(q, kn_ref[...])
    row = past + lax.broadcasted_iota(jnp.int32, (ts, ts), 0)
    col = past + lax.broadcasted_iota(jnp.int32, (ts, ts), 1)
    sn = jnp.where(col < (row // CHUNK + 1) * CHUNK, sn, NEG_BIG)
    m = jnp.maximum(jnp.max(sp, axis=-1, keepdims=True), jnp.max(sn, axis=-1, keepdims=True))
    pp = jnp.exp2(sp - m)
    pn = jnp.exp2(sn - m)
    l = jnp.sum(_lane_fold(pp, jnp.add), axis=-1, keepdims=True) + jnp.sum(pn, axis=-1, keepdims=True)
    o = _dot(pp.astype(BF16), vp_ref[...]) + _dot(pn.astype(BF16), vn_ref[:, :V_DIM])
    o_ref[...] = (o / l).astype(o_ref.dtype)


def _attn_sample(q, k_past, kpe_past, v_past, k_new, v_new, mix, *, bs, ts, past, row0):
    r, mw = mix.shape
    col0 = (mw // 2) // V_DIM
    g0 = row0 // ts
    return pl.pallas_call(
        functools.partial(_attn_sample_kernel, past=past),
        out_shape=jax.ShapeDtypeStruct((r, mw), mix.dtype),
        grid=(bs, MLA_HEADS),
        in_specs=[
            pl.BlockSpec((None, ts, 2 * LANE), lambda b, h: (h, g0 + b, 0)),
            pl.BlockSpec((None, past, QK_NOPE), lambda b, h: (h, b, 0)),
            pl.BlockSpec((past, LANE), lambda b, h: (b, 0)),
            pl.BlockSpec((None, past, V_DIM), lambda b, h: (h, b, 0)),
            pl.BlockSpec((None, ts, 2 * LANE), lambda b, h: (h, g0 + b, 0)),
            pl.BlockSpec((None, ts, 2 * V_DIM), lambda b, h: (h, g0 + b, 0)),
            pl.BlockSpec(memory_space=pl.ANY),
        ],
        out_specs=pl.BlockSpec((ts, V_DIM), lambda b, h: (g0 + b, col0 + h)),
        input_output_aliases={6: 0},
        compiler_params=_cparams(("arbitrary", "arbitrary")),
        name="attn_sample",
    )(q, k_past, kpe_past, v_past, k_new, v_new, mix)


def _ret_kernel(*refs, has_init, has_alias):
    refs = list(refs)
    q_ref, k_ref, v_ref, g_ref = refs[:4]
    del refs[:4]
    if has_init:
        s0_ref = refs.pop(0)
    dm_ref, qd_ref, kd_ref, cd_ref = refs[:4]
    del refs[:4]
    if has_alias:
        del refs[:1]
    o_ref, s_ref = refs
    hpg, kdim, vdim = s_ref.shape

    @pl.when(pl.program_id(2) == 0)
    def _():
        if has_init:
            s_ref[...] = s0_ref[...]
        else:
            s_ref[...] = jnp.zeros_like(s_ref)

    for h in range(hpg):
        q = q_ref[:, h * kdim:(h + 1) * kdim]
        k = k_ref[:, h * kdim:(h + 1) * kdim]
        v = v_ref[:, h * vdim:(h + 1) * vdim]
        s = s_ref[h]
        att = _dot_nt(q, k) * dm_ref[h]
        o = _dot(att.astype(BF16), v) + _dot(q, s.astype(BF16)) * qd_ref[:, h:h + 1]
        kdec = (k.astype(F32) * kd_ref[:, h:h + 1]).astype(BF16)
        s_ref[h] = s * cd_ref[:, h:h + 1] + _dot_tn(kdec, v)
        mu = jnp.mean(o, axis=-1, keepdims=True)
        oc = o - mu
        var = jnp.mean(oc * oc, axis=-1, keepdims=True)
        g = g_ref[:, h * vdim:(h + 1) * vdim].astype(F32)
        o_ref[:, h * vdim:(h + 1) * vdim] = (_silu(g) * (oc * lax.rsqrt(var + EPS))).astype(o_ref.dtype)


def _retention_tables(heads, hpg, L):
    lg = jnp.log1p(-jnp.exp2(-5.0 - jnp.arange(heads, dtype=F32)))
    idx = jnp.arange(L, dtype=F32)
    rel = idx[:, None] - idx[None, :]
    dmask = jnp.exp(jnp.where(rel[None] >= 0, rel[None] * lg[:, None, None], -jnp.inf))
    grp = lambda a: a.reshape(a.shape[0], heads // hpg, hpg).transpose(1, 0, 2)
    qdec = grp(jnp.exp((idx[:, None] + 1.0) * lg[None, :]))
    kdec = grp(jnp.exp((L - 1.0 - idx[:, None]) * lg[None, :]))
    cdec = grp(jnp.exp(L * lg)[None, :])
    return dmask, qdec, kdec, cdec


def _retention(qk, vg, *, L, nseq, ncs, row0, heads, s0=None, o_prev=None):
    r = qk.shape[0]
    kdim = qk.shape[1] // (2 * heads)
    vdim = vg.shape[1] // (2 * heads)
    hpg = 4
    nhg = heads // hpg
    dmask, qdec, kdec, cdec = _retention_tables(heads, hpg, L)
    rb0 = row0 // L
    rowblk = lambda s, c: rb0 + s * ncs + c
    in_specs = [
        pl.BlockSpec((L, hpg * kdim), lambda hg, s, c: (rowblk(s, c), hg)),
        pl.BlockSpec((L, hpg * kdim), lambda hg, s, c: (rowblk(s, c), nhg + hg)),
        pl.BlockSpec((L, hpg * vdim), lambda hg, s, c: (rowblk(s, c), hg)),
        pl.BlockSpec((L, hpg * vdim), lambda hg, s, c: (rowblk(s, c), nhg + hg)),
    ]
    args = [qk, qk, vg, vg]
    if s0 is not None:
        in_specs.append(pl.BlockSpec((None, hpg, kdim, vdim), lambda hg, s, c: (s, hg, 0, 0)))
        args.append(s0)
    in_specs += [pl.BlockSpec((hpg, L, L), lambda hg, s, c: (hg, 0, 0)),
                 pl.BlockSpec((None, L, hpg), lambda hg, s, c: (hg, 0, 0)),
                 pl.BlockSpec((None, L, hpg), lambda hg, s, c: (hg, 0, 0)),
                 pl.BlockSpec((None, 1, hpg), lambda hg, s, c: (hg, 0, 0))]
    args += [dmask, qdec, kdec, cdec]
    aliases = {}
    if o_prev is not None:
        in_specs.append(pl.BlockSpec(memory_space=pl.ANY))
        aliases = {len(args): 0}
        args.append(o_prev)
    return pl.pallas_call(
        functools.partial(_ret_kernel, has_init=s0 is not None, has_alias=o_prev is not None),
        out_shape=(jax.ShapeDtypeStruct((r, heads * vdim), BF16),
                   jax.ShapeDtypeStruct((nseq, heads, kdim, vdim), F32)),
        grid=(nhg, nseq, ncs),
        in_specs=in_specs,
        out_specs=(pl.BlockSpec((L, hpg * vdim), lambda hg, s, c: (rowblk(s, c), hg)),
                   pl.BlockSpec((None, hpg, kdim, vdim), lambda hg, s, c: (s, hg, 0, 0))),
        input_output_aliases=aliases,
        compiler_params=_cparams(("arbitrary", "arbitrary", "arbitrary")),
        name="retention",
    )(*args)


def _pad_rope_cols(a):
    half = QK_ROPE // 2
    z = jnp.zeros(a.shape[:-1] + (LANE // 2 - half,), a.dtype)
    return jnp.concatenate([a[..., :half], z, a[..., half:], z], axis=-1)


def _unpad_rope_cols(a):
    half = QK_ROPE // 2
    return jnp.concatenate([a[..., :half], a[..., LANE // 2:LANE // 2 + half]], axis=-1)


def _rope_tables(pos, half):
    inv = ROPE_THETA ** (-jnp.arange(half, dtype=F32) / half)
    ang = pos.astype(F32)[:, None] * inv[None, :]
    return jnp.cos(ang), jnp.sin(ang)


@jax.jit
def kernel(x_prompt, x_sample, c_prompt, c_sample, cache_mla_ckv, cache_mla_kpe, state_ssd, state_ssd_conv, state_ret, norm_mix, norm_ffn, w_ada, b_ada, w_in0, conv_w, conv_b, dt_bias, a_log, d_skip, ssd_norm, q_a_norm, w_uq, q_norm_nope, q_norm_rope, kv_a_norm, w_ukv, k_norm_nope, k_norm_rope, w_out0, w_in1, w_out1, w_gate, w_up, w_down):
    bp, tp, d = x_prompt.shape
    bs, ts, _ = x_sample.shape
    past = cache_mla_ckv.shape[2]
    assert ts == CHUNK and tp % CHUNK == 0 and past % CHUNK == 0
    rp, rs = bp * tp, bs * ts
    r = rp + rs
    ncp = tp // CHUNK
    bm = min(1024, r)
    assert r % bm == 0 and bm % 512 == 0

    ssd_width = d // 2
    ssd_heads = ssd_width // SSD_HEADDIM
    conv_ch = ssd_width + 2 * SSD_GROUPS * SSD_STATE
    off_dt = ssd_width + conv_ch
    off_cq = off_dt + ssd_heads
    off_ckv = off_cq + Q_LORA
    off_kpe = off_ckv + KV_LORA
    assert conv_ch == 2 * ssd_width and ssd_heads <= LANE
    mix0 = ssd_width + MLA_HEADS * V_DIM
    ret_kdim = d // RET_HEADS
    ret_qk = RET_HEADS * ret_kdim

    wi = w_in0[0]
    w0_main = jnp.concatenate([wi[:, :off_dt], wi[:, off_cq:off_kpe]], axis=1).astype(BF16)
    w0_small = jnp.concatenate([wi[:, off_dt:off_cq], jnp.zeros((d, LANE - ssd_heads), F32),
                                _pad_rope_cols(wi[:, off_kpe:])], axis=1).astype(BF16)
    wq = w_uq[0].reshape(Q_LORA, MLA_HEADS, QK_NOPE + QK_ROPE)
    w_uq_p = jnp.concatenate([wq[:, :, :QK_NOPE].reshape(Q_LORA, -1),
                              _pad_rope_cols(wq[:, :, QK_NOPE:]).reshape(Q_LORA, -1)], axis=1).astype(BF16)
    wkv = w_ukv[0].reshape(KV_LORA, MLA_HEADS, QK_NOPE + V_DIM)
    w_ukv_p = jnp.concatenate([wkv[:, :, :QK_NOPE].reshape(KV_LORA, -1),
                               wkv[:, :, QK_NOPE:].reshape(KV_LORA, -1)], axis=1).astype(BF16)
    w_out0_b = w_out0[0].astype(BF16)
    w1_b = w_in1[0].astype(BF16)
    w_out1_b = w_out1[0].astype(BF16)
    w_gate_b, w_up_b, w_down_b = w_gate.astype(BF16), w_up.astype(BF16), w_down.astype(BF16)

    def lane_pad(a, n=LANE):
        return jnp.pad(a, (0, n - a.shape[0])).reshape(1, n)

    pos = jnp.concatenate([jnp.tile(jnp.arange(tp), bp), jnp.tile(past + jnp.arange(ts), bs)])
    c32, s32 = _rope_tables(pos, QK_ROPE // 2)
    zq = jnp.zeros_like(c32)
    cos_m = jnp.concatenate([c32, zq, c32, zq], axis=1)
    sin_m = jnp.concatenate([-s32, zq, s32, zq], axis=1)
    cos_r, sin_r = _rope_tables(pos, ret_kdim // 2)
    lp = min(256, tp)

    nb = -(-(bp + bs) // 8) * 8
    c_all = jnp.concatenate([c_prompt, c_sample, jnp.zeros((nb - bp - bs, d), F32)], axis=0)
    mod = _ada(c_all, w_ada, b_ada)
    depth = mod.shape[0]
    mod = mod.reshape(depth, -1, 6, d).transpose(0, 2, 1, 3)
    mod_p = jnp.broadcast_to(mod[:, :, :bp, None, :], (depth, 6, bp, ncp, d)).reshape(depth, 6, bp * ncp, d)
    tbl = jnp.concatenate([mod_p, mod[:, :, bp:bp + bs]], axis=2)

    x_groups = ((x_prompt.reshape(rp, d), 0), (x_sample.reshape(rs, d), rp))
    bm_res = bm if rp % bm == 0 and rs % bm == 0 else 512
    assert rp % bm_res == 0 and rs % bm_res == 0
    bm_wide = r // 8
    assert bm_wide % 16 == 0

    def ffn(x, i, split_out=False):
        hn = _normmod(x, norm_ffn[i], tbl[i, 4], tbl[i, 3])
        hid = _mm(hn, [w_gate_b, w_up_b], layer=i, bm=bm_wide, bn=256, out_dtype=BF16, epi=_epi_swiglu,
                  name="ffn_gate_up")
        down = functools.partial(_residual_mm, hid, w_down_b, x, tbl[i, 5], layer=i, bm=512, bn=512,
                                 name="ffn_down")
        if not split_out:
            return down()
        return [down(rows=n, x_row0=row0, res_row0=row0) for row0, n in ((0, rp), (rp, rs))]

    hn = None
    for x_part, row0 in x_groups:
        hn = _normmod(x_part, norm_mix[0], tbl[0, 1], tbl[0, 0], out_rows=r, row0=row0, prev=hn)
    proj = _mm(hn, [w0_main], bm=bm, bn=768, out_dtype=F32, epi=_epi_plain, name="in_proj0")
    small = _mm(hn, [w0_small], bm=bm, bn=2 * LANE, out_dtype=F32, epi=_epi_plain, name="in_proj0_small")

    ssd_params = (conv_w[0], conv_b[0].reshape(1, -1), lane_pad(dt_bias[0]), lane_pad(a_log[0]),
                  lane_pad(d_skip[0]), ssd_norm[0].reshape(1, -1))
    mix, h_p, conv_p8 = _ssd(proj, small, ssd_params, L=lp, nseq=bp, ncs=tp // lp, row0=0, mix_width=mix0)
    mix, h_s, conv_s8 = _ssd(proj, small, ssd_params, L=ts, nseq=bs, ncs=1, row0=rp, mix_width=mix0,
                             init=(state_ssd_conv[0], state_ssd[0].reshape(bs, ssd_width, SSD_STATE)), mix=mix)

    gr_q = _pad_rope_cols(q_norm_rope[0]).reshape(1, LANE)
    gr_k = _pad_rope_cols(k_norm_rope[0]).reshape(1, LANE)
    cq_block = (off_dt + 0) // Q_LORA
    ckv_block = (off_dt + Q_LORA) // KV_LORA
    q_all = _mla_q(proj, cq_block, q_a_norm[0].reshape(1, -1), w_uq_p, q_norm_nope[0].reshape(1, -1), gr_q,
                   cos_m, sin_m)
    ckv_all, kpe_all = _mla_ckv(proj, small, ckv_block, kv_a_norm[0].reshape(1, -1), gr_k, cos_m, sin_m)
    gk = k_norm_nope[0].reshape(1, -1)
    k_all, v_all = _mla_kvup(ckv_all, kpe_all, w_ukv_p, gk)
    k_past, v_past = _mla_kvup(cache_mla_ckv[0].reshape(bs * past, KV_LORA), None, w_ukv_p, gk)
    kpe_past = _pad_rope_cols(cache_mla_kpe[0].reshape(bs * past, QK_ROPE)).astype(BF16)
    mix = _attn_prompt(q_all, k_all, v_all, mix, bp=bp, tp=tp)
    mix = _attn_sample(q_all, k_past, kpe_past, v_past, k_all, v_all, mix, bs=bs, ts=ts, past=past, row0=rp)

    x = None
    for x_part, row0 in x_groups:
        x = _residual_mm(mix, w_out0_b, x_part, tbl[0, 2], bm=bm_res, bn=512, name="out_proj0",
                         rows=x_part.shape[0], x_row0=row0, out_rows=r, out_row0=row0, prev=x)
    x = ffn(x, 0)

    hn = _normmod(x, norm_mix[1], tbl[1, 1], tbl[1, 0])
    hpb = 1024 // ret_kdim
    rope_epi = functools.partial(_epi_rope_qk, head_dim=ret_kdim, k_block0=RET_HEADS // hpb,
                                 k_scale=ret_kdim ** -0.5)
    half = ret_kdim // 2
    qk = _mm(hn, [w1_b], n=2 * ret_qk, bm=bm, bn=1024, out_dtype=BF16, epi=rope_epi,
             extras=[(cos_r, (bm, half), lambda i, j: (i, 0)), (sin_r, (bm, half), lambda i, j: (i, 0))],
             name="in_proj1_qk")
    vg = _mm(hn, [w1_b], n=w1_b.shape[1] - 2 * ret_qk, w_col0=2 * ret_qk, bm=bm, bn=1024, out_dtype=BF16,
             epi=_epi_plain, name="in_proj1_vg")
    o_ret, ret_p = _retention(qk, vg, L=lp, nseq=bp, ncs=tp // lp, row0=0, heads=RET_HEADS)
    o_ret, ret_s = _retention(qk, vg, L=ts, nseq=bs, ncs=1, row0=rp, heads=RET_HEADS, s0=state_ret[0],
                              o_prev=o_ret)
    x = _residual_mm(o_ret, w_out1_b, x, tbl[1, 2], bm=bm, bn=1024, bk=2048, name="out_proj1")
    y_p, y_s = ffn(x, 1, split_out=True)

    def split(a, shape_p, shape_s, n=rp):
        return a[:n].reshape(shape_p), a[n:].reshape(shape_s)

    y_p, y_s = y_p.reshape(bp, tp, d), y_s.reshape(bs, ts, d)
    ckv_p, ckv_s = split(ckv_all, (1, bp, tp, KV_LORA), (1, bs, ts, KV_LORA))
    kpe_p, kpe_s = split(_unpad_rope_cols(kpe_all), (1, bp, tp, QK_ROPE), (1, bs, ts, QK_ROPE))
    hshape = (ssd_heads, SSD_HEADDIM, SSD_STATE)
    ssd_p, ssd_s = h_p.reshape((1, bp) + hshape), h_s.reshape((1, bs) + hshape)
    conv_p, conv_s = conv_p8[None, :, 8 - (CONV_W - 1):], conv_s8[None, :, 8 - (CONV_W - 1):]
    return (y_p, y_s, ckv_p, kpe_p, ssd_p, conv_p, ret_p[None], ckv_s, kpe_s, ssd_s, conv_s, ret_s[None])
```

```python
import functools
import math

import jax
import jax.numpy as jnp
from jax import lax
from jax.experimental import pallas as pl
from jax.experimental.pallas import tpu as pltpu

F32 = jnp.float32
BF16 = jnp.bfloat16

CHUNK = 64
EPS = 1e-6
NEG_BIG = -1e30

SSD_HEADDIM = 64
SSD_GROUPS = 8
SSD_STATE = 128
CONV_W = 4
MLA_HEADS = 16
QK_NOPE = 128
QK_ROPE = 64
V_DIM = 128
Q_LORA = 1024
KV_LORA = 512
ROPE_THETA = 10000.0
MLA_SCALE = (QK_NOPE + QK_ROPE) ** -0.5
Q_PRESCALE = MLA_SCALE * math.log2(math.e)
RET_HEADS = 16
LANE = 128

VMEM_LIMIT = 56 * 1024 * 1024


def _cparams(sem):
    return pltpu.CompilerParams(dimension_semantics=sem, vmem_limit_bytes=VMEM_LIMIT)


def _sigmoid(x):
    return 1.0 / (1.0 + jnp.exp(-x))


def _silu(x):
    return x * _sigmoid(x)


def _softplus(x):
    return jnp.maximum(x, 0.0) + jnp.log1p(jnp.exp(-jnp.abs(x)))


def _dot(a, b):
    return jnp.dot(a, b, preferred_element_type=F32)


def _dot_nt(a, b):
    return lax.dot_general(a, b, (((1,), (1,)), ((), ())), preferred_element_type=F32)


def _dot_tn(a, b):
    return lax.dot_general(a, b, (((0,), (0,)), ((), ())), preferred_element_type=F32)


def _ada_kernel(c_ref, w_ref, b_ref, o_ref):
    c = c_ref[...]
    o_ref[...] = _dot(_silu(c).astype(BF16), w_ref[...].astype(BF16)) + b_ref[...]


def _ada(c_all, w_ada, b_ada):
    depth, d, n = w_ada.shape
    nb = c_all.shape[0]
    tn = 512
    return pl.pallas_call(
        _ada_kernel,
        out_shape=jax.ShapeDtypeStruct((depth, nb, n), F32),
        grid=(depth, n // tn),
        in_specs=[
            pl.BlockSpec((nb, d), lambda l, j: (0, 0)),
            pl.BlockSpec((None, d, tn), lambda l, j: (l, 0, j)),
            pl.BlockSpec((None, 1, tn), lambda l, j: (l, 0, j)),
        ],
        out_specs=pl.BlockSpec((None, nb, tn), lambda l, j: (l, 0, j)),
        compiler_params=_cparams(("arbitrary", "arbitrary")),
        name="ada",
    )(c_all, w_ada, b_ada.reshape(depth, 1, n))


def _normmod_kernel(x_ref, g_ref, sc_ref, sh_ref, *rest):
    o_ref = rest[-1]
    x = x_ref[...]
    rb, d = x.shape
    y = x * lax.rsqrt(jnp.mean(x * x, axis=-1, keepdims=True) + EPS) * g_ref[...]
    y = y.reshape(rb // CHUNK, CHUNK, d)
    y = y * (1.0 + sc_ref[...][:, None, :]) + sh_ref[...][:, None, :]
    o_ref[...] = y.reshape(rb, d).astype(o_ref.dtype)


def _normmod(x, gain, sc_tbl, sh_tbl, *, out_rows=None, row0=0, prev=None):
    r, d = x.shape
    rb = 512
    gb = rb // CHUNK
    b0 = row0 // rb
    out_rows = r if out_rows is None else out_rows
    in_specs = [
        pl.BlockSpec((rb, d), lambda i: (i, 0)),
        pl.BlockSpec((1, d), lambda i: (0, 0)),
        pl.BlockSpec((gb, d), lambda i: (b0 + i, 0)),
        pl.BlockSpec((gb, d), lambda i: (b0 + i, 0)),
    ]
    args = [x, gain.reshape(1, d), sc_tbl, sh_tbl]
    aliases = {}
    if prev is not None:
        in_specs.append(pl.BlockSpec(memory_space=pl.ANY))
        args.append(prev)
        aliases = {4: 0}
    return pl.pallas_call(
        _normmod_kernel,
        out_shape=jax.ShapeDtypeStruct((out_rows, d), BF16),
        grid=(r // rb,),
        in_specs=in_specs,
        out_specs=pl.BlockSpec((rb, d), lambda i: (b0 + i, 0)),
        input_output_aliases=aliases,
        compiler_params=_cparams(("arbitrary",)),
        name="normmod",
    )(*args)


def _mm_kernel(*refs, n_w, n_ex, nk, epi, has_alias):
    x_ref = refs[0]
    w_refs = refs[1:1 + n_w]
    ex_refs = refs[1 + n_w:1 + n_w + n_ex]
    refs = refs[1 + n_w + n_ex + (1 if has_alias else 0):]
    o_ref = refs[0]
    acc_refs = refs[1:]
    j = pl.program_id(1)
    if nk == 1:
        accs = [_dot(x_ref[...], w[...]) for w in w_refs]
        o_ref[...] = epi(accs, ex_refs, j).astype(o_ref.dtype)
    else:
        k = pl.program_id(2)

        @pl.when(k == 0)
        def _():
            for a in acc_refs:
                a[...] = jnp.zeros_like(a)

        for a, w in zip(acc_refs, w_refs):
            a[...] += _dot(x_ref[...], w[...])

        @pl.when(k == nk - 1)
        def _():
            o_ref[...] = epi([a[...] for a in acc_refs], ex_refs, j).astype(o_ref.dtype)


def _mm(x, ws, *, bm, bn, bk=None, out_dtype, epi, extras=(), name, n=None, w_col0=0, layer=None, rows=None,
        x_row0=0, out_rows=None, out_row0=0, out_cols=None, out_col0=0, prev=None):
    kdim = x.shape[1]
    rows = x.shape[0] if rows is None else rows
    n = ws[0].shape[-1] if n is None else n
    out_rows = rows if out_rows is None else out_rows
    bk = kdim if bk is None else bk
    nk = kdim // bk
    xb, ob, wb, ocb = x_row0 // bm, out_row0 // bm, w_col0 // bn, out_col0 // bn
    out_cols = n if out_cols is None else out_cols
    in_specs = [pl.BlockSpec((bm, bk), lambda i, j, k: (xb + i, k))]
    if layer is None:
        in_specs += [pl.BlockSpec((bk, bn), lambda i, j, k: (k, wb + j)) for _ in ws]
    else:
        in_specs += [pl.BlockSpec((None, bk, bn), lambda i, j, k: (layer, k, wb + j)) for _ in ws]
    for _, bs, im in extras:
        in_specs.append(pl.BlockSpec(bs, functools.partial(lambda i, j, k, im: im(i, j), im=im)))
    args = [x, *ws, *[e[0] for e in extras]]
    aliases = {}
    if prev is not None:
        in_specs.append(pl.BlockSpec(memory_space=pl.ANY))
        aliases = {len(args): 0}
        args.append(prev)
    scratch = [pltpu.VMEM((bm, bn), F32) for _ in ws] if nk > 1 else []
    return pl.pallas_call(
        functools.partial(_mm_kernel, n_w=len(ws), n_ex=len(extras), nk=nk, epi=epi, has_alias=prev is not None),
        out_shape=jax.ShapeDtypeStruct((out_rows, out_cols), out_dtype),
        grid=(rows // bm, n // bn, nk),
        in_specs=in_specs,
        out_specs=pl.BlockSpec((bm, bn), lambda i, j, k: (ob + i, ocb + j)),
        scratch_shapes=scratch,
        input_output_aliases=aliases,
        compiler_params=_cparams(("arbitrary", "arbitrary", "arbitrary")),
        name=name,
    )(*args)


def _epi_plain(accs, ex, j):
    return accs[0]


def _epi_swiglu(accs, ex, j):
    return _silu(accs[0]) * accs[1]


def _epi_residual(accs, ex, j):
    res_ref, gate_ref = ex
    acc = accs[0]
    bm, bn = acc.shape
    upd = acc.reshape(bm // CHUNK, CHUNK, bn) * gate_ref[...][:, None, :]
    return res_ref[...] + upd.reshape(bm, bn)


def _epi_rope_qk(accs, ex, j, *, head_dim, k_block0, k_scale):
    cos_ref, sin_ref = ex
    acc = accs[0]
    c, s = cos_ref[...], sin_ref[...]
    half = head_dim // 2
    scale = jnp.where(j >= k_block0, k_scale, 1.0).astype(F32)
    outs = []
    for h in range(acc.shape[1] // head_dim):
        x1 = acc[:, h * head_dim:h * head_dim + half]
        x2 = acc[:, h * head_dim + half:(h + 1) * head_dim]
        outs.append((x1 * c - x2 * s) * scale)
        outs.append((x1 * s + x2 * c) * scale)
    return jnp.concatenate(outs, axis=-1)


def _residual_mm(x, w, res, gate_tbl, *, bm, bn, bk=None, name, layer=None, rows=None, x_row0=0, res_row0=0,
                 out_rows=None, out_row0=0, prev=None):
    gb = bm // CHUNK
    xb, rb = x_row0 // bm, res_row0 // bm
    return _mm(x, [w], bm=bm, bn=bn, bk=bk, out_dtype=F32, epi=_epi_residual,
               extras=[(res, (bm, bn), lambda i, j: (rb + i, j)), (gate_tbl, (gb, bn), lambda i, j: (xb + i, j))],
               name=name, layer=layer, rows=rows, x_row0=x_row0, out_rows=out_rows, out_row0=out_row0, prev=prev)


def _ssd_kernel(*refs, has_init, has_alias):
    refs = list(refs)
    z_ref, xa_ref, xb_ref, sm_ref = refs[:4]
    del refs[:4]
    if has_init:
        cinit_ref, h0_ref = refs[:2]
        del refs[:2]
    cw_ref, cb_ref, dtb_ref, alog_ref, dsk_ref, nrm_ref = refs[:6]
    del refs[:6]
    if has_alias:
        del refs[:1]
    y_ref, hout_ref, cout_ref, ext, xc, ysc = refs
    L, width = xa_ref.shape
    heads = width // SSD_HEADDIM
    rpg = heads // SSD_GROUPS
    gw = rpg * SSD_HEADDIM

    @pl.when(pl.program_id(1) == 0)
    def _():
        if has_init:
            ext[5:8, :] = cinit_ref[...]
            hout_ref[...] = h0_ref[...]
        else:
            ext[0:8, :] = jnp.zeros((8, ext.shape[1]), F32)
            hout_ref[...] = jnp.zeros_like(hout_ref)

    ext[8:8 + L, 0:width] = xa_ref[...]
    ext[8:8 + L, width:2 * width] = xb_ref[...]

    cstep = 512
    for c in range(2 * width // cstep):
        sl = slice(c * cstep, (c + 1) * cstep)
        a = cb_ref[:, sl] + cw_ref[0:1, sl] * ext[5:5 + L, sl]
        for w in range(1, CONV_W):
            a = a + cw_ref[w:w + 1, sl] * ext[5 + w:5 + w + L, sl]
        xc[:, sl] = _silu(a)
    tail_rows = ext[L:L + 8, :]
    cout_ref[...] = tail_rows
    ext[0:8, :] = tail_rows

    dt = _softplus(sm_ref[:, 0:LANE] + dtb_ref[...])
    a_neg = -jnp.exp(alog_ref[...])
    cs = dt * a_neg
    row = lax.broadcasted_iota(jnp.int32, (L, LANE), 0)
    sh = 1
    while sh < L:
        cs = cs + jnp.where(row >= sh, pltpu.roll(cs, sh, axis=0), 0.0)
        sh *= 2
    both_t = jnp.concatenate([cs, dt], axis=0).T
    cs_t = both_t[:, 0:L]
    dt_t = both_t[:, L:2 * L]
    tail_t = jnp.exp(cs_t[:, L - 1:L] - cs_t) * dt_t
    ecs = jnp.exp(cs)

    ii = lax.broadcasted_iota(jnp.int32, (L, L), 0)
    jj = lax.broadcasted_iota(jnp.int32, (L, L), 1)
    causal = ii >= jj

    half = width // 2
    x_t = jnp.concatenate([xc[:, 0:half], xc[:, half:width]], axis=0).T

    for g in range(SSD_GROUPS):
        bg = xc[:, width + g * SSD_STATE:width + (g + 1) * SSD_STATE].astype(BF16)
        cg = xc[:, width + SSD_GROUPS * SSD_STATE + g * SSD_STATE:
                width + SSD_GROUPS * SSD_STATE + (g + 1) * SSD_STATE].astype(BF16)
        cbm = _dot_nt(cg, bg)
        hg = hout_ref[g * gw:(g + 1) * gw, :]
        ystate = _dot_nt(cg, hg.astype(BF16))
        lo = (g * gw) % half
        lanes = slice(0, L) if g * gw < half else slice(L, 2 * L)
        xs_rows, dec_rows = [], []
        for r in range(rpg):
            h = g * rpg + r
            cs_col = cs[:, h:h + 1]
            seg = cs_col - cs_t[h:h + 1, :]
            dec = jnp.exp(jnp.where(causal, seg, NEG_BIG))
            wts = cbm * dec * dt_t[h:h + 1, :]
            xh = xc[:, h * SSD_HEADDIM:(h + 1) * SSD_HEADDIM]
            yh = _dot(wts.astype(BF16), xh.astype(BF16))
            yh = yh + ystate[:, r * SSD_HEADDIM:(r + 1) * SSD_HEADDIM] * ecs[:, h:h + 1]
            yh = yh + xh * dsk_ref[:, h:h + 1]
            ysc[:, h * SSD_HEADDIM:(h + 1) * SSD_HEADDIM] = yh
            xs_rows.append(x_t[lo + r * SSD_HEADDIM:lo + (r + 1) * SSD_HEADDIM, lanes] * tail_t[h:h + 1, :])
            dec_rows.append(jnp.broadcast_to(jnp.exp(cs_t[h:h + 1, L - 1:L]), (SSD_HEADDIM, SSD_STATE)))
        upd = _dot(jnp.concatenate(xs_rows, axis=0).astype(BF16), bg)
        hout_ref[g * gw:(g + 1) * gw, :] = hg * jnp.concatenate(dec_rows, axis=0) + upd

    y = ysc[...] * _silu(z_ref[...])
    outs = []
    for g in range(SSD_GROUPS):
        yg = y[:, g * gw:(g + 1) * gw]
        ms = jnp.mean(yg * yg, axis=-1, keepdims=True)
        outs.append(yg * lax.rsqrt(ms + EPS) * nrm_ref[:, g * gw:(g + 1) * gw])
    y_ref[...] = jnp.concatenate(outs, axis=-1).astype(y_ref.dtype)


def _ssd(proj, small, params, *, L, nseq, ncs, row0, mix_width, init=None, mix=None):
    conv_w, conv_b, dtb, alog, dsk, nrm = params
    r = proj.shape[0]
    width = nrm.shape[1]
    rb0 = row0 // L
    rowblk = lambda s, c: rb0 + s * ncs + c
    const = lambda s, c: (0, 0)
    in_specs = [
        pl.BlockSpec((L, width), lambda s, c: (rowblk(s, c), 0)),
        pl.BlockSpec((L, width), lambda s, c: (rowblk(s, c), 1)),
        pl.BlockSpec((L, width), lambda s, c: (rowblk(s, c), 2)),
        pl.BlockSpec((L, 2 * LANE), lambda s, c: (rowblk(s, c), 0)),
    ]
    args = [proj, proj, proj, small]
    if init is not None:
        in_specs += [pl.BlockSpec((None, CONV_W - 1, 2 * width), lambda s, c: (s, 0, 0)),
                     pl.BlockSpec((None, width, SSD_STATE), lambda s, c: (s, 0, 0))]
        args += list(init)
    in_specs += [pl.BlockSpec((CONV_W, 2 * width), const), pl.BlockSpec((1, 2 * width), const),
                 pl.BlockSpec((1, LANE), const), pl.BlockSpec((1, LANE), const), pl.BlockSpec((1, LANE), const),
                 pl.BlockSpec((1, width), const)]
    args += [conv_w, conv_b, dtb, alog, dsk, nrm]
    aliases = {}
    if mix is not None:
        in_specs.append(pl.BlockSpec(memory_space=pl.ANY))
        aliases = {len(args): 0}
        args.append(mix)
    return pl.pallas_call(
        functools.partial(_ssd_kernel, has_init=init is not None, has_alias=mix is not None),
        out_shape=(jax.ShapeDtypeStruct((r, mix_width), BF16),
                   jax.ShapeDtypeStruct((nseq, width, SSD_STATE), F32),
                   jax.ShapeDtypeStruct((nseq, 8, 2 * width), F32)),
        grid=(nseq, ncs),
        in_specs=in_specs,
        out_specs=(
            pl.BlockSpec((L, width), lambda s, c: (rowblk(s, c), 0)),
            pl.BlockSpec((None, width, SSD_STATE), lambda s, c: (s, 0, 0)),
            pl.BlockSpec((None, 8, 2 * width), lambda s, c: (s, 0, 0)),
        ),
        scratch_shapes=[pltpu.VMEM((L + 8, 2 * width), F32), pltpu.VMEM((L, 2 * width), F32),
                        pltpu.VMEM((L, width), F32)],
        input_output_aliases=aliases,
        compiler_params=_cparams(("arbitrary", "arbitrary")),
        name="ssd",
    )(*args)


def _rope_pad(x, cos_t, sin_t):
    return x * cos_t + pltpu.roll(x, LANE // 2, axis=1) * sin_t


def _mla_q_kernel(cq_ref, ga_ref, w_ref, gn_ref, gr_ref, cos_ref, sin_ref, q_ref):
    cq = cq_ref[...]
    xn = cq * lax.rsqrt(jnp.mean(cq * cq, axis=-1, keepdims=True) + EPS) * ga_ref[...]
    q = _dot(xn.astype(BF16), w_ref[...])
    c, s = cos_ref[...], sin_ref[...]
    nope_w = MLA_HEADS * QK_NOPE
    for h in range(MLA_HEADS):
        qn = q[:, h * QK_NOPE:(h + 1) * QK_NOPE]
        qn = qn * lax.rsqrt(jnp.mean(qn * qn, axis=-1, keepdims=True) + EPS) * gn_ref[...]
        qp = q[:, nope_w + h * LANE:nope_w + (h + 1) * LANE]
        qp = qp * lax.rsqrt(jnp.sum(qp * qp, axis=-1, keepdims=True) * (1.0 / QK_ROPE) + EPS) * gr_ref[...]
        qp = _rope_pad(qp, c, s)
        q_ref[h] = (jnp.concatenate([qn, qp], axis=-1) * Q_PRESCALE).astype(q_ref.dtype)


def _mla_q(proj, cq_block, ga, w_uq_p, gn, gr_pad, cos_t, sin_t):
    r = proj.shape[0]
    bm = 512
    n = w_uq_p.shape[1]
    return pl.pallas_call(
        _mla_q_kernel,
        out_shape=jax.ShapeDtypeStruct((MLA_HEADS, r, 2 * LANE), BF16),
        grid=(r // bm,),
        in_specs=[
            pl.BlockSpec((bm, Q_LORA), lambda i: (i, cq_block)),
            pl.BlockSpec((1, Q_LORA), lambda i: (0, 0)),
            pl.BlockSpec((Q_LORA, n), lambda i: (0, 0)),
            pl.BlockSpec((1, QK_NOPE), lambda i: (0, 0)),
            pl.BlockSpec((1, LANE), lambda i: (0, 0)),
            pl.BlockSpec((bm, LANE), lambda i: (i, 0)),
            pl.BlockSpec((bm, LANE), lambda i: (i, 0)),
        ],
        out_specs=pl.BlockSpec((MLA_HEADS, bm, 2 * LANE), lambda i: (0, i, 0)),
        compiler_params=_cparams(("arbitrary",)),
        name="mla_q",
    )(proj, ga, w_uq_p, gn, gr_pad, cos_t, sin_t)


def _mla_ckv_kernel(ckv_ref, sm_ref, gkv_ref, gr_ref, cos_ref, sin_ref, ckv_out, kpe_out):
    x = ckv_ref[...]
    ckv_out[...] = x * lax.rsqrt(jnp.mean(x * x, axis=-1, keepdims=True) + EPS) * gkv_ref[...]
    kp = sm_ref[:, LANE:2 * LANE]
    kp = kp * lax.rsqrt(jnp.sum(kp * kp, axis=-1, keepdims=True) * (1.0 / QK_ROPE) + EPS) * gr_ref[...]
    kpe_out[...] = _rope_pad(kp, cos_ref[...], sin_ref[...])


def _mla_ckv(proj, small, ckv_block, gkv, gr_pad, cos_t, sin_t):
    r = proj.shape[0]
    bm = 512
    return pl.pallas_call(
        _mla_ckv_kernel,
        out_shape=(jax.ShapeDtypeStruct((r, KV_LORA), F32), jax.ShapeDtypeStruct((r, LANE), F32)),
        grid=(r // bm,),
        in_specs=[
            pl.BlockSpec((bm, KV_LORA), lambda i: (i, ckv_block)),
            pl.BlockSpec((bm, 2 * LANE), lambda i: (i, 0)),
            pl.BlockSpec((1, KV_LORA), lambda i: (0, 0)),
            pl.BlockSpec((1, LANE), lambda i: (0, 0)),
            pl.BlockSpec((bm, LANE), lambda i: (i, 0)),
            pl.BlockSpec((bm, LANE), lambda i: (i, 0)),
        ],
        out_specs=(pl.BlockSpec((bm, KV_LORA), lambda i: (i, 0)), pl.BlockSpec((bm, LANE), lambda i: (i, 0))),
        compiler_params=_cparams(("arbitrary",)),
        name="mla_ckv",
    )(proj, small, gkv, gr_pad, cos_t, sin_t)


def _mla_kvup_kernel(*refs, compact):
    if compact:
        ckv_ref, w_ref, gk_ref, k_ref, v_ref = refs
    else:
        ckv_ref, kpe_ref, w_ref, gk_ref, k_ref, v_ref = refs
    kv = _dot(ckv_ref[...].astype(BF16), w_ref[...])
    nope_w = MLA_HEADS * QK_NOPE
    for h in range(MLA_HEADS):
        kn = kv[:, h * QK_NOPE:(h + 1) * QK_NOPE]
        kn = kn * lax.rsqrt(jnp.mean(kn * kn, axis=-1, keepdims=True) + EPS) * gk_ref[...]
        vh = kv[:, nope_w + h * V_DIM:nope_w + (h + 1) * V_DIM]
        if not compact:
            kn = jnp.concatenate([kn, kpe_ref[...]], axis=-1)
            vh = jnp.concatenate([vh, jnp.ones_like(vh)], axis=-1)
        k_ref[h] = kn.astype(k_ref.dtype)
        v_ref[h] = vh.astype(v_ref.dtype)


def _mla_kvup(ckv, kpe_pad, w_ukv_p, gk):
    r = ckv.shape[0]
    bm = 512
    n = w_ukv_p.shape[1]
    compact = kpe_pad is None
    kw, vw = (QK_NOPE, V_DIM) if compact else (2 * LANE, 2 * V_DIM)
    in_specs = [pl.BlockSpec((bm, KV_LORA), lambda i: (i, 0))]
    args = [ckv]
    if not compact:
        in_specs.append(pl.BlockSpec((bm, LANE), lambda i: (i, 0)))
        args.append(kpe_pad)
    in_specs += [pl.BlockSpec((KV_LORA, n), lambda i: (0, 0)), pl.BlockSpec((1, QK_NOPE), lambda i: (0, 0))]
    args += [w_ukv_p, gk]
    return pl.pallas_call(
        functools.partial(_mla_kvup_kernel, compact=compact),
        out_shape=(jax.ShapeDtypeStruct((MLA_HEADS, r, kw), BF16),
                   jax.ShapeDtypeStruct((MLA_HEADS, r, vw), BF16)),
        grid=(r // bm,),
        in_specs=in_specs,
        out_specs=(pl.BlockSpec((MLA_HEADS, bm, kw), lambda i: (0, i, 0)),
                   pl.BlockSpec((MLA_HEADS, bm, vw), lambda i: (0, i, 0))),
        compiler_params=_cparams(("arbitrary",)),
        name="mla_kvup",
    )(*args)


def _lane_fold(x, op):
    out = x[:, 0:LANE]
    for c in range(1, x.shape[1] // LANE):
        out = op(out, x[:, c * LANE:(c + 1) * LANE])
    return out


def _attn_prompt_kernel(q_ref, k_ref, v_ref, mix_ref, o_ref, s_scr, *, tq):
    del mix_ref
    tp = q_ref.shape[0]
    ri = lax.broadcasted_iota(jnp.int32, (tq, tq), 0) // CHUNK
    ci = lax.broadcasted_iota(jnp.int32, (tq, tq), 1) // CHUNK
    diag_visible = ci <= ri

    for qi in range(tp // tq):
        q = q_ref[qi * tq:(qi + 1) * tq, :]

        macc = None
        for kb in range(qi + 1):
            s = _dot_nt(q, k_ref[kb * tq:(kb + 1) * tq, :])
            if kb == qi:
                s = jnp.where(diag_visible, s, NEG_BIG)
            s_scr[kb] = s
            fold = _lane_fold(s, jnp.maximum)
            macc = fold if macc is None else jnp.maximum(macc, fold)
        m = jnp.max(macc, axis=-1, keepdims=True)

        acc = None
        for kb in range(qi + 1):
            p = jnp.exp2(s_scr[kb] - m)
            pv = _dot(p.astype(BF16), v_ref[kb * tq:(kb + 1) * tq, :])
            acc = pv if acc is None else acc + pv
        o_ref[qi * tq:(qi + 1) * tq, :] = (acc[:, :V_DIM] / acc[:, V_DIM:]).astype(o_ref.dtype)


def _attn_prompt(q, k, v, mix, *, bp, tp):
    tq = min(512, tp)
    r, mw = mix.shape
    col0 = (mw // 2) // V_DIM
    return pl.pallas_call(
        functools.partial(_attn_prompt_kernel, tq=tq),
        out_shape=jax.ShapeDtypeStruct((r, mw), mix.dtype),
        grid=(bp, MLA_HEADS),
        in_specs=[
            pl.BlockSpec((None, tp, 2 * LANE), lambda b, h: (h, b, 0)),
            pl.BlockSpec((None, tp, 2 * LANE), lambda b, h: (h, b, 0)),
            pl.BlockSpec((None, tp, 2 * V_DIM), lambda b, h: (h, b, 0)),
            pl.BlockSpec(memory_space=pl.ANY),
        ],
        out_specs=pl.BlockSpec((tp, V_DIM), lambda b, h: (b, col0 + h)),
        scratch_shapes=[pltpu.VMEM((tp // tq, tq, tq), F32)],
        input_output_aliases={3: 0},
        compiler_params=_cparams(("arbitrary", "arbitrary")),
        name="attn_prompt",
    )(q, k, v, mix)


def _attn_sample_kernel(q_ref, kp_ref, kpe_ref, vp_ref, kn_ref, vn_ref, mix_ref, o_ref, *, past):
    del mix_ref
    q = q_ref[...]
    ts = q.shape[0]
    k_past = jnp.concatenate([kp_ref[...], kpe_ref[...]], axis=-1)
    sp = _dot_nt(q, k_past)
    sn = _dot_nt(q, kn_ref[...])
    row = past + lax.broadcasted_iota(jnp.int32, (ts, ts), 0)
    col = past + lax.broadcasted_iota(jnp.int32, (ts, ts), 1)
    sn = jnp.where(col < (row // CHUNK + 1) * CHUNK, sn, NEG_BIG)
    m = jnp.maximum(jnp.max(sp, axis=-1, keepdims=True), jnp.max(sn, axis=-1, keepdims=True))
    pp = jnp.exp2(sp - m)
    pn = jnp.exp2(sn - m)
    l = jnp.sum(_lane_fold(pp, jnp.add), axis=-1, keepdims=True) + jnp.sum(pn, axis=-1, keepdims=True)
    o = _dot(pp.astype(BF16), vp_ref[...]) + _dot(pn.astype(BF16), vn_ref[:, :V_DIM])
    o_ref[...] = (o / l).astype(o_ref.dtype)


def _attn_sample(q, k_past, kpe_past, v_past, k_new, v_new, mix, *, bs, ts, past, row0):
    r, mw = mix.shape
    col0 = (mw // 2) // V_DIM
    g0 = row0 // ts
    return pl.pallas_call(
        functools.partial(_attn_sample_kernel, past=past),
        out_shape=jax.ShapeDtypeStruct((r, mw), mix.dtype),
        grid=(bs, MLA_HEADS),
        in_specs=[
            pl.BlockSpec((None, ts, 2 * LANE), lambda b, h: (h, g0 + b, 0)),
            pl.BlockSpec((None, past, QK_NOPE), lambda b, h: (h, b, 0)),
            pl.BlockSpec((past, LANE), lambda b, h: (b, 0)),
            pl.BlockSpec((None, past, V_DIM), lambda b, h: (h, b, 0)),
            pl.BlockSpec((None, ts, 2 * LANE), lambda b, h: (h, g0 + b, 0)),
            pl.BlockSpec((None, ts, 2 * V_DIM), lambda b, h: (h, g0 + b, 0)),
            pl.BlockSpec(memory_space=pl.ANY),
        ],
        out_specs=pl.BlockSpec((ts, V_DIM), lambda b, h: (g0 + b, col0 + h)),
        input_output_aliases={6: 0},
        compiler_params=_cparams(("arbitrary", "arbitrary")),
        name="attn_sample",
    )(q, k_past, kpe_past, v_past, k_new, v_new, mix)


def _ret_kernel(*refs, has_init, has_alias):
    refs = list(refs)
    q_ref, k_ref, v_ref, g_ref = refs[:4]
    del refs[:4]
    if has_init:
        s0_ref = refs.pop(0)
    dm_ref, qd_ref, kd_ref, cd_ref = refs[:4]
    del refs[:4]
    if has_alias:
        del refs[:1]
    o_ref, s_ref = refs
    hpg, kdim, vdim = s_ref.shape

    @pl.when(pl.program_id(2) == 0)
    def _():
        if has_init:
            s_ref[...] = s0_ref[...]
        else:
            s_ref[...] = jnp.zeros_like(s_ref)

    for h in range(hpg):
        q = q_ref[:, h * kdim:(h + 1) * kdim]
        k = k_ref[:, h * kdim:(h + 1) * kdim]
        v = v_ref[:, h * vdim:(h + 1) * vdim]
        s = s_ref[h]
        att = _dot_nt(q, k) * dm_ref[h]
        o = _dot(att.astype(BF16), v) + _dot(q, s.astype(BF16)) * qd_ref[:, h:h + 1]
        kdec = (k.astype(F32) * kd_ref[:, h:h + 1]).astype(BF16)
        s_ref[h] = s * cd_ref[:, h:h + 1] + _dot_tn(kdec, v)
        mu = jnp.mean(o, axis=-1, keepdims=True)
        oc = o - mu
        var = jnp.mean(oc * oc, axis=-1, keepdims=True)
        g = g_ref[:, h * vdim:(h + 1) * vdim].astype(F32)
        o_ref[:, h * vdim:(h + 1) * vdim] = (_silu(g) * (oc * lax.rsqrt(var + EPS))).astype(o_ref.dtype)


def _retention_tables(heads, hpg, L):
    lg = jnp.log1p(-jnp.exp2(-5.0 - jnp.arange(heads, dtype=F32)))
    idx = jnp.arange(L, dtype=F32)
    rel = idx[:, None] - idx[None, :]
    dmask = jnp.exp(jnp.where(rel[None] >= 0, rel[None] * lg[:, None, None], -jnp.inf))
    grp = lambda a: a.reshape(a.shape[0], heads // hpg, hpg).transpose(1, 0, 2)
    qdec = grp(jnp.exp((idx[:, None] + 1.0) * lg[None, :]))
    kdec = grp(jnp.exp((L - 1.0 - idx[:, None]) * lg[None, :]))
    cdec = grp(jnp.exp(L * lg)[None, :])
    return dmask, qdec, kdec, cdec


def _retention(qk, vg, *, L, nseq, ncs, row0, heads, s0=None, o_prev=None):
    r = qk.shape[0]
    kdim = qk.shape[1] // (2 * heads)
    vdim = vg.shape[1] // (2 * heads)
    hpg = 4
    nhg = heads // hpg
    dmask, qdec, kdec, cdec = _retention_tables(heads, hpg, L)
    rb0 = row0 // L
    rowblk = lambda s, c: rb0 + s * ncs + c
    in_specs = [
        pl.BlockSpec((L, hpg * kdim), lambda hg, s, c: (rowblk(s, c), hg)),
        pl.BlockSpec((L, hpg * kdim), lambda hg, s, c: (rowblk(s, c), nhg + hg)),
        pl.BlockSpec((L, hpg * vdim), lambda hg, s, c: (rowblk(s, c), hg)),
        pl.BlockSpec((L, hpg * vdim), lambda hg, s, c: (rowblk(s, c), nhg + hg)),
    ]
    args = [qk, qk, vg, vg]
    if s0 is not None:
        in_specs.append(pl.BlockSpec((None, hpg, kdim, vdim), lambda hg, s, c: (s, hg, 0, 0)))
        args.append(s0)
    in_specs += [pl.BlockSpec((hpg, L, L), lambda hg, s, c: (hg, 0, 0)),
                 pl.BlockSpec((None, L, hpg), lambda hg, s, c: (hg, 0, 0)),
                 pl.BlockSpec((None, L, hpg), lambda hg, s, c: (hg, 0, 0)),
                 pl.BlockSpec((None, 1, hpg), lambda hg, s, c: (hg, 0, 0))]
    args += [dmask, qdec, kdec, cdec]
    aliases = {}
    if o_prev is not None:
        in_specs.append(pl.BlockSpec(memory_space=pl.ANY))
        aliases = {len(args): 0}
        args.append(o_prev)
    return pl.pallas_call(
        functools.partial(_ret_kernel, has_init=s0 is not None, has_alias=o_prev is not None),
        out_shape=(jax.ShapeDtypeStruct((r, heads * vdim), BF16),
                   jax.ShapeDtypeStruct((nseq, heads, kdim, vdim), F32)),
        grid=(nhg, nseq, ncs),
        in_specs=in_specs,
        out_specs=(pl.BlockSpec((L, hpg * vdim), lambda hg, s, c: (rowblk(s, c), hg)),
                   pl.BlockSpec((None, hpg, kdim, vdim), lambda hg, s, c: (s, hg, 0, 0))),
        input_output_aliases=aliases,
        compiler_params=_cparams(("arbitrary", "arbitrary", "arbitrary")),
        name="retention",
    )(*args)


def _pad_rope_cols(a):
    half = QK_ROPE // 2
    z = jnp.zeros(a.shape[:-1] + (LANE // 2 - half,), a.dtype)
    return jnp.concatenate([a[..., :half], z, a[..., half:], z], axis=-1)


def _unpad_rope_cols(a):
    half = QK_ROPE // 2
    return jnp.concatenate([a[..., :half], a[..., LANE // 2:LANE // 2 + half]], axis=-1)


def _rope_tables(pos, half):
    inv = ROPE_THETA ** (-jnp.arange(half, dtype=F32) / half)
    ang = pos.astype(F32)[:, None] * inv[None, :]
    return jnp.cos(ang), jnp.sin(ang)


@jax.jit
def kernel(x_prompt, x_sample, c_prompt, c_sample, cache_mla_ckv, cache_mla_kpe, state_ssd, state_ssd_conv, state_ret, norm_mix, norm_ffn, w_ada, b_ada, w_in0, conv_w, conv_b, dt_bias, a_log, d_skip, ssd_norm, q_a_norm, w_uq, q_norm_nope, q_norm_rope, kv_a_norm, w_ukv, k_norm_nope, k_norm_rope, w_out0, w_in1, w_out1, w_gate, w_up, w_down):
    bp, tp, d = x_prompt.shape
    bs, ts, _ = x_sample.shape
    past = cache_mla_ckv.shape[2]
    assert ts == CHUNK and tp % CHUNK == 0 and past % CHUNK == 0
    rp, rs = bp * tp, bs * ts
    r = rp + rs
    ncp = tp // CHUNK
    bm = min(1024, r)
    assert r % bm == 0 and bm % 512 == 0

    ssd_width = d // 2
    ssd_heads = ssd_width // SSD_HEADDIM
    conv_ch = ssd_width + 2 * SSD_GROUPS * SSD_STATE
    off_dt = ssd_width + conv_ch
    off_cq = off_dt + ssd_heads
    off_ckv = off_cq + Q_LORA
    off_kpe = off_ckv + KV_LORA
    assert conv_ch == 2 * ssd_width and ssd_heads <= LANE
    mix0 = ssd_width + MLA_HEADS * V_DIM
    ret_kdim = d // RET_HEADS
    ret_qk = RET_HEADS * ret_kdim

    wi = w_in0[0]
    w0_main = jnp.concatenate([wi[:, :off_dt], wi[:, off_cq:off_kpe]], axis=1).astype(BF16)
    w0_small = jnp.concatenate([wi[:, off_dt:off_cq], jnp.zeros((d, LANE - ssd_heads), F32),
                                _pad_rope_cols(wi[:, off_kpe:])], axis=1).astype(BF16)
    wq = w_uq[0].reshape(Q_LORA, MLA_HEADS, QK_NOPE + QK_ROPE)
    w_uq_p = jnp.concatenate([wq[:, :, :QK_NOPE].reshape(Q_LORA, -1),
                              _pad_rope_cols(wq[:, :, QK_NOPE:]).reshape(Q_LORA, -1)], axis=1).astype(BF16)
    wkv = w_ukv[0].reshape(KV_LORA, MLA_HEADS, QK_NOPE + V_DIM)
    w_ukv_p = jnp.concatenate([wkv[:, :, :QK_NOPE].reshape(KV_LORA, -1),
                               wkv[:, :, QK_NOPE:].reshape(KV_LORA, -1)], axis=1).astype(BF16)
    w_out0_b = w_out0[0].astype(BF16)
    w1_b = w_in1[0].astype(BF16)
    w_out1_b = w_out1[0].astype(BF16)
    w_gate_b, w_up_b, w_down_b = w_gate.astype(BF16), w_up.astype(BF16), w_down.astype(BF16)

    def lane_pad(a, n=LANE):
        return jnp.pad(a, (0, n - a.shape[0])).reshape(1, n)

    pos = jnp.concatenate([jnp.tile(jnp.arange(tp), bp), jnp.tile(past + jnp.arange(ts), bs)])
    c32, s32 = _rope_tables(pos, QK_ROPE // 2)
    zq = jnp.zeros_like(c32)
    cos_m = jnp.concatenate([c32, zq, c32, zq], axis=1)
    sin_m = jnp.concatenate([-s32, zq, s32, zq], axis=1)
    cos_r, sin_r = _rope_tables(pos, ret_kdim // 2)
    lp = min(256, tp)

    nb = -(-(bp + bs) // 8) * 8
    c_all = jnp.concatenate([c_prompt, c_sample, jnp.zeros((nb - bp - bs, d), F32)], axis=0)
    mod = _ada(c_all, w_ada, b_ada)
    depth = mod.shape[0]
    mod = mod.reshape(depth, -1, 6, d).transpose(0, 2, 1, 3)
    mod_p = jnp.broadcast_to(mod[:, :, :bp, None, :], (depth, 6, bp, ncp, d)).reshape(depth, 6, bp * ncp, d)
    tbl = jnp.concatenate([mod_p, mod[:, :, bp:bp + bs]], axis=2)

    x_groups = ((x_prompt.reshape(rp, d), 0), (x_sample.reshape(rs, d), rp))
    bm_res = bm if rp % bm == 0 and rs % bm == 0 else 512
    assert rp % bm_res == 0 and rs % bm_res == 0
    bm_wide = r // 8
    assert bm_wide % 16 == 0

    def ffn(x, i, split_out=False):
        hn = _normmod(x, norm_ffn[i], tbl[i, 4], tbl[i, 3])
        d_ff = w_gate_b.shape[2]
        n_main = d_ff // 512 * 512
        gate_up = functools.partial(_mm, hn, [w_gate_b, w_up_b], layer=i, out_dtype=BF16, epi=_epi_swiglu,
                                    out_cols=d_ff, name="ffn_gate_up")
        hid = gate_up(n=n_main, bm=bm, bn=512)
        if n_main < d_ff:
            hid = gate_up(n=d_ff - n_main, w_col0=n_main, out_col0=n_main, bm=bm_wide, bn=256, prev=hid)
        down = functools.partial(_residual_mm, hid, w_down_b, x, tbl[i, 5], layer=i, bm=512, bn=512,
                                 name="ffn_down")
        if not split_out:
            return down()
        return [down(rows=n, x_row0=row0, res_row0=row0) for row0, n in ((0, rp), (rp, rs))]

    hn = None
    for x_part, row0 in x_groups:
        hn = _normmod(x_part, norm_mix[0], tbl[0, 1], tbl[0, 0], out_rows=r, row0=row0, prev=hn)
    proj = _mm(hn, [w0_main], bm=bm, bn=768, out_dtype=F32, epi=_epi_plain, name="in_proj0")
    small = _mm(hn, [w0_small], bm=bm, bn=2 * LANE, out_dtype=F32, epi=_epi_plain, name="in_proj0_small")

    ssd_params = (conv_w[0], conv_b[0].reshape(1, -1), lane_pad(dt_bias[0]), lane_pad(a_log[0]),
                  lane_pad(d_skip[0]), ssd_norm[0].reshape(1, -1))
    mix, h_p, conv_p8 = _ssd(proj, small, ssd_params, L=lp, nseq=bp, ncs=tp // lp, row0=0, mix_width=mix0)
    mix, h_s, conv_s8 = _ssd(proj, small, ssd_params, L=ts, nseq=bs, ncs=1, row0=rp, mix_width=mix0,
                             init=(state_ssd_conv[0], state_ssd[0].reshape(bs, ssd_width, SSD_STATE)), mix=mix)

    gr_q = _pad_rope_cols(q_norm_rope[0]).reshape(1, LANE)
    gr_k = _pad_rope_cols(k_norm_rope[0]).reshape(1, LANE)
    cq_block = (off_dt + 0) // Q_LORA
    ckv_block = (off_dt + Q_LORA) // KV_LORA
    q_all = _mla_q(proj, cq_block, q_a_norm[0].reshape(1, -1), w_uq_p, q_norm_nope[0].reshape(1, -1), gr_q,
                   cos_m, sin_m)
    ckv_all, kpe_all = _mla_ckv(proj, small, ckv_block, kv_a_norm[0].reshape(1, -1), gr_k, cos_m, sin_m)
    gk = k_norm_nope[0].reshape(1, -1)
    k_all, v_all = _mla_kvup(ckv_all, kpe_all, w_ukv_p, gk)
    k_past, v_past = _mla_kvup(cache_mla_ckv[0].reshape(bs * past, KV_LORA), None, w_ukv_p, gk)
    kpe_past = _pad_rope_cols(cache_mla_kpe[0].reshape(bs * past, QK_ROPE)).astype(BF16)
    mix = _attn_prompt(q_all, k_all, v_all, mix, bp=bp, tp=tp)
    mix = _attn_sample(q_all, k_past, kpe_past, v_past, k_all, v_all, mix, bs=bs, ts=ts, past=past, row0=rp)

    x = None
    for x_part, row0 in x_groups:
        x = _residual_mm(mix, w_out0_b, x_part, tbl[0, 2], bm=bm_res, bn=512, name="out_proj0",
                         rows=x_part.shape[0], x_row0=row0, out_rows=r, out_row0=row0, prev=x)
    x = ffn(x, 0)

    hn = _normmod(x, norm_mix[1], tbl[1, 1], tbl[1, 0])
    hpb = 1024 // ret_kdim
    rope_epi = functools.partial(_epi_rope_qk, head_dim=ret_kdim, k_block0=RET_HEADS // hpb,
                                 k_scale=ret_kdim ** -0.5)
    half = ret_kdim // 2
    qk = _mm(hn, [w1_b], n=2 * ret_qk, bm=bm, bn=1024, out_dtype=BF16, epi=rope_epi,
             extras=[(cos_r, (bm, half), lambda i, j: (i, 0)), (sin_r, (bm, half), lambda i, j: (i, 0))],
             name="in_proj1_qk")
    vg = _mm(hn, [w1_b], n=w1_b.shape[1] - 2 * ret_qk, w_col0=2 * ret_qk, bm=bm, bn=1024, out_dtype=BF16,
             epi=_epi_plain, name="in_proj1_vg")
    o_ret, ret_p = _retention(qk, vg, L=lp, nseq=bp, ncs=tp // lp, row0=0, heads=RET_HEADS)
    o_ret, ret_s = _retention(qk, vg, L=ts, nseq=bs, ncs=1, row0=rp, heads=RET_HEADS, s0=state_ret[0],
                              o_prev=o_ret)
    x = _residual_mm(o_ret, w_out1_b, x, tbl[1, 2], bm=bm, bn=1024, bk=2048, name="out_proj1")
    y_p, y_s = ffn(x, 1, split_out=True)

    def split(a, shape_p, shape_s, n=rp):
        return a[:n].reshape(shape_p), a[n:].reshape(shape_s)

    y_p, y_s = y_p.reshape(bp, tp, d), y_s.reshape(bs, ts, d)
    ckv_p, ckv_s = split(ckv_all, (1, bp, tp, KV_LORA), (1, bs, ts, KV_LORA))
    kpe_p, kpe_s = split(_unpad_rope_cols(kpe_all), (1, bp, tp, QK_ROPE), (1, bs, ts, QK_ROPE))
    hshape = (ssd_heads, SSD_HEADDIM, SSD_STATE)
    ssd_p, ssd_s = h_p.reshape((1, bp) + hshape), h_s.reshape((1, bs) + hshape)
    conv_p, conv_s = conv_p8[None, :, 8 - (CONV_W - 1):], conv_s8[None, :, 8 - (CONV_W - 1):]
    return (y_p, y_s, ckv_p, kpe_p, ssd_p, conv_p, ret_p[None], ckv_s, kpe_s, ssd_s, conv_s, ret_s[None])
```

```python
import functools
import math

import jax
import jax.numpy as jnp
from jax import lax
from jax.experimental import pallas as pl
from jax.experimental.pallas import tpu as pltpu

F32 = jnp.float32
BF16 = jnp.bfloat16

CHUNK = 64
EPS = 1e-6
NEG_BIG = -1e30

SSD_HEADDIM = 64
SSD_GROUPS = 8
SSD_STATE = 128
CONV_W = 4
MLA_HEADS = 16
QK_NOPE = 128
QK_ROPE = 64
V_DIM = 128
Q_LORA = 1024
KV_LORA = 512
ROPE_THETA = 10000.0
MLA_SCALE = (QK_NOPE + QK_ROPE) ** -0.5
Q_PRESCALE = MLA_SCALE * math.log2(math.e)
RET_HEADS = 16
LANE = 128

VMEM_LIMIT = 56 * 1024 * 1024


def _cparams(sem):
    return pltpu.CompilerParams(dimension_semantics=sem, vmem_limit_bytes=VMEM_LIMIT)


def _sigmoid(x):
    return 1.0 / (1.0 + jnp.exp(-x))


def _silu(x):
    return x * _sigmoid(x)


def _softplus(x):
    return jnp.maximum(x, 0.0) + jnp.log1p(jnp.exp(-jnp.abs(x)))


def _dot(a, b):
    return jnp.dot(a, b, preferred_element_type=F32)


def _dot_nt(a, b):
    return lax.dot_general(a, b, (((1,), (1,)), ((), ())), preferred_element_type=F32)


def _dot_tn(a, b):
    return lax.dot_general(a, b, (((0,), (0,)), ((), ())), preferred_element_type=F32)


def _ada_kernel(c_ref, w_ref, b_ref, o_ref):
    c = c_ref[...]
    o_ref[...] = _dot(_silu(c).astype(BF16), w_ref[...].astype(BF16)) + b_ref[...]


def _ada(c_all, w_ada, b_ada):
    depth, d, n = w_ada.shape
    nb = c_all.shape[0]
    tn = 512
    return pl.pallas_call(
        _ada_kernel,
        out_shape=jax.ShapeDtypeStruct((depth, nb, n), F32),
        grid=(depth, n // tn),
        in_specs=[
            pl.BlockSpec((nb, d), lambda l, j: (0, 0)),
            pl.BlockSpec((None, d, tn), lambda l, j: (l, 0, j)),
            pl.BlockSpec((None, 1, tn), lambda l, j: (l, 0, j)),
        ],
        out_specs=pl.BlockSpec((None, nb, tn), lambda l, j: (l, 0, j)),
        compiler_params=_cparams(("arbitrary", "arbitrary")),
        name="ada",
    )(c_all, w_ada, b_ada.reshape(depth, 1, n))


def _normmod_kernel(x_ref, g_ref, sc_ref, sh_ref, *rest):
    o_ref = rest[-1]
    x = x_ref[...]
    rb, d = x.shape
    y = x * lax.rsqrt(jnp.mean(x * x, axis=-1, keepdims=True) + EPS) * g_ref[...]
    y = y.reshape(rb // CHUNK, CHUNK, d)
    y = y * (1.0 + sc_ref[...][:, None, :]) + sh_ref[...][:, None, :]
    o_ref[...] = y.reshape(rb, d).astype(o_ref.dtype)


def _normmod(x, gain, sc_tbl, sh_tbl, *, out_rows=None, row0=0, prev=None):
    r, d = x.shape
    rb = 512
    gb = rb // CHUNK
    b0 = row0 // rb
    out_rows = r if out_rows is None else out_rows
    in_specs = [
        pl.BlockSpec((rb, d), lambda i: (i, 0)),
        pl.BlockSpec((1, d), lambda i: (0, 0)),
        pl.BlockSpec((gb, d), lambda i: (b0 + i, 0)),
        pl.BlockSpec((gb, d), lambda i: (b0 + i, 0)),
    ]
    args = [x, gain.reshape(1, d), sc_tbl, sh_tbl]
    aliases = {}
    if prev is not None:
        in_specs.append(pl.BlockSpec(memory_space=pl.ANY))
        args.append(prev)
        aliases = {4: 0}
    return pl.pallas_call(
        _normmod_kernel,
        out_shape=jax.ShapeDtypeStruct((out_rows, d), BF16),
        grid=(r // rb,),
        in_specs=in_specs,
        out_specs=pl.BlockSpec((rb, d), lambda i: (b0 + i, 0)),
        input_output_aliases=aliases,
        compiler_params=_cparams(("arbitrary",)),
        name="normmod",
    )(*args)


def _mm_kernel(*refs, n_w, n_ex, nk, epi, has_alias):
    x_ref = refs[0]
    w_refs = refs[1:1 + n_w]
    ex_refs = refs[1 + n_w:1 + n_w + n_ex]
    refs = refs[1 + n_w + n_ex + (1 if has_alias else 0):]
    o_ref = refs[0]
    acc_refs = refs[1:]
    j = pl.program_id(1)
    if nk == 1:
        accs = [_dot(x_ref[...], w[...]) for w in w_refs]
        o_ref[...] = epi(accs, ex_refs, j).astype(o_ref.dtype)
    else:
        k = pl.program_id(2)

        @pl.when(k == 0)
        def _():
            for a in acc_refs:
                a[...] = jnp.zeros_like(a)

        for a, w in zip(acc_refs, w_refs):
            a[...] += _dot(x_ref[...], w[...])

        @pl.when(k == nk - 1)
        def _():
            o_ref[...] = epi([a[...] for a in acc_refs], ex_refs, j).astype(o_ref.dtype)


def _mm(x, ws, *, bm, bn, bk=None, out_dtype, epi, extras=(), name, n=None, w_col0=0, layer=None, rows=None,
        x_row0=0, out_rows=None, out_row0=0, out_cols=None, out_col0=0, prev=None):
    kdim = x.shape[1]
    rows = x.shape[0] if rows is None else rows
    n = ws[0].shape[-1] if n is None else n
    out_rows = rows if out_rows is None else out_rows
    bk = kdim if bk is None else bk
    nk = kdim // bk
    xb, ob, wb, ocb = x_row0 // bm, out_row0 // bm, w_col0 // bn, out_col0 // bn
    out_cols = n if out_cols is None else out_cols
    in_specs = [pl.BlockSpec((bm, bk), lambda i, j, k: (xb + i, k))]
    if layer is None:
        in_specs += [pl.BlockSpec((bk, bn), lambda i, j, k: (k, wb + j)) for _ in ws]
    else:
        in_specs += [pl.BlockSpec((None, bk, bn), lambda i, j, k: (layer, k, wb + j)) for _ in ws]
    for _, bs, im in extras:
        in_specs.append(pl.BlockSpec(bs, functools.partial(lambda i, j, k, im: im(i, j), im=im)))
    args = [x, *ws, *[e[0] for e in extras]]
    aliases = {}
    if prev is not None:
        in_specs.append(pl.BlockSpec(memory_space=pl.ANY))
        aliases = {len(args): 0}
        args.append(prev)
    scratch = [pltpu.VMEM((bm, bn), F32) for _ in ws] if nk > 1 else []
    return pl.pallas_call(
        functools.partial(_mm_kernel, n_w=len(ws), n_ex=len(extras), nk=nk, epi=epi, has_alias=prev is not None),
        out_shape=jax.ShapeDtypeStruct((out_rows, out_cols), out_dtype),
        grid=(rows // bm, n // bn, nk),
        in_specs=in_specs,
        out_specs=pl.BlockSpec((bm, bn), lambda i, j, k: (ob + i, ocb + j)),
        scratch_shapes=scratch,
        input_output_aliases=aliases,
        compiler_params=_cparams(("arbitrary", "arbitrary", "arbitrary")),
        name=name,
    )(*args)


def _epi_plain(accs, ex, j):
    return accs[0]


def _epi_swiglu(accs, ex, j):
    return _silu(accs[0]) * accs[1]


def _epi_residual(accs, ex, j):
    res_ref, gate_ref = ex
    acc = accs[0]
    bm, bn = acc.shape
    upd = acc.reshape(bm // CHUNK, CHUNK, bn) * gate_ref[...][:, None, :]
    return res_ref[...] + upd.reshape(bm, bn)


def _epi_rope_qk(accs, ex, j, *, head_dim, k_block0, k_scale):
    cos_ref, sin_ref = ex
    acc = accs[0]
    c, s = cos_ref[...], sin_ref[...]
    half = head_dim // 2
    scale = jnp.where(j >= k_block0, k_scale, 1.0).astype(F32)
    outs = []
    for h in range(acc.shape[1] // head_dim):
        x1 = acc[:, h * head_dim:h * head_dim + half]
        x2 = acc[:, h * head_dim + half:(h + 1) * head_dim]
        outs.append((x1 * c - x2 * s) * scale)
        outs.append((x1 * s + x2 * c) * scale)
    return jnp.concatenate(outs, axis=-1)


def _residual_mm(x, w, res, gate_tbl, *, bm, bn, bk=None, name, layer=None, rows=None, x_row0=0, res_row0=0,
                 out_rows=None, out_row0=0, prev=None):
    gb = bm // CHUNK
    xb, rb = x_row0 // bm, res_row0 // bm
    return _mm(x, [w], bm=bm, bn=bn, bk=bk, out_dtype=F32, epi=_epi_residual,
               extras=[(res, (bm, bn), lambda i, j: (rb + i, j)), (gate_tbl, (gb, bn), lambda i, j: (xb + i, j))],
               name=name, layer=layer, rows=rows, x_row0=x_row0, out_rows=out_rows, out_row0=out_row0, prev=prev)


def _ssd_kernel(*refs, has_init, has_alias):
    refs = list(refs)
    z_ref, xa_ref, xb_ref, sm_ref = refs[:4]
    del refs[:4]
    if has_init:
        cinit_ref, h0_ref = refs[:2]
        del refs[:2]
    cw_ref, cb_ref, dtb_ref, alog_ref, dsk_ref, nrm_ref = refs[:6]
    del refs[:6]
    if has_alias:
        del refs[:1]
    y_ref, hout_ref, cout_ref, ext, xc, ysc = refs
    L, width = xa_ref.shape
    heads = width // SSD_HEADDIM
    rpg = heads // SSD_GROUPS
    gw = rpg * SSD_HEADDIM

    @pl.when(pl.program_id(1) == 0)
    def _():
        if has_init:
            ext[5:8, :] = cinit_ref[...]
            hout_ref[...] = h0_ref[...]
        else:
            ext[0:8, :] = jnp.zeros((8, ext.shape[1]), F32)
            hout_ref[...] = jnp.zeros_like(hout_ref)

    ext[8:8 + L, 0:width] = xa_ref[...]
    ext[8:8 + L, width:2 * width] = xb_ref[...]

    cstep = 512
    for c in range(2 * width // cstep):
        sl = slice(c * cstep, (c + 1) * cstep)
        a = cb_ref[:, sl] + cw_ref[0:1, sl] * ext[5:5 + L, sl]
        for w in range(1, CONV_W):
            a = a + cw_ref[w:w + 1, sl] * ext[5 + w:5 + w + L, sl]
        xc[:, sl] = _silu(a)
    tail_rows = ext[L:L + 8, :]
    cout_ref[...] = tail_rows
    ext[0:8, :] = tail_rows

    dt = _softplus(sm_ref[:, 0:LANE] + dtb_ref[...])
    a_neg = -jnp.exp(alog_ref[...])
    cs = dt * a_neg
    row = lax.broadcasted_iota(jnp.int32, (L, LANE), 0)
    sh = 1
    while sh < L:
        cs = cs + jnp.where(row >= sh, pltpu.roll(cs, sh, axis=0), 0.0)
        sh *= 2
    both_t = jnp.concatenate([cs, dt], axis=0).T
    cs_t = both_t[:, 0:L]
    dt_t = both_t[:, L:2 * L]
    tail_t = jnp.exp(cs_t[:, L - 1:L] - cs_t) * dt_t
    ecs = jnp.exp(cs)

    ii = lax.broadcasted_iota(jnp.int32, (L, L), 0)
    jj = lax.broadcasted_iota(jnp.int32, (L, L), 1)
    causal = ii >= jj

    half = width // 2
    x_t = jnp.concatenate([xc[:, 0:half], xc[:, half:width]], axis=0).T

    for g in range(SSD_GROUPS):
        bg = xc[:, width + g * SSD_STATE:width + (g + 1) * SSD_STATE].astype(BF16)
        cg = xc[:, width + SSD_GROUPS * SSD_STATE + g * SSD_STATE:
                width + SSD_GROUPS * SSD_STATE + (g + 1) * SSD_STATE].astype(BF16)
        cbm = _dot_nt(cg, bg)
        hg = hout_ref[g * gw:(g + 1) * gw, :]
        ystate = _dot_nt(cg, hg.astype(BF16))
        lo = (g * gw) % half
        lanes = slice(0, L) if g * gw < half else slice(L, 2 * L)
        xs_rows, dec_rows = [], []
        for r in range(rpg):
            h = g * rpg + r
            cs_col = cs[:, h:h + 1]
            seg = cs_col - cs_t[h:h + 1, :]
            dec = jnp.exp(jnp.where(causal, seg, NEG_BIG))
            wts = cbm * dec * dt_t[h:h + 1, :]
            xh = xc[:, h * SSD_HEADDIM:(h + 1) * SSD_HEADDIM]
            yh = _dot(wts.astype(BF16), xh.astype(BF16))
            yh = yh + ystate[:, r * SSD_HEADDIM:(r + 1) * SSD_HEADDIM] * ecs[:, h:h + 1]
            yh = yh + xh * dsk_ref[:, h:h + 1]
            ysc[:, h * SSD_HEADDIM:(h + 1) * SSD_HEADDIM] = yh
            xs_rows.append(x_t[lo + r * SSD_HEADDIM:lo + (r + 1) * SSD_HEADDIM, lanes] * tail_t[h:h + 1, :])
            dec_rows.append(jnp.broadcast_to(jnp.exp(cs_t[h:h + 1, L - 1:L]), (SSD_HEADDIM, SSD_STATE)))
        upd = _dot(jnp.concatenate(xs_rows, axis=0).astype(BF16), bg)
        hout_ref[g * gw:(g + 1) * gw, :] = hg * jnp.concatenate(dec_rows, axis=0) + upd

    y = ysc[...] * _silu(z_ref[...])
    outs = []
    for g in range(SSD_GROUPS):
        yg = y[:, g * gw:(g + 1) * gw]
        ms = jnp.mean(yg * yg, axis=-1, keepdims=True)
        outs.append(yg * lax.rsqrt(ms + EPS) * nrm_ref[:, g * gw:(g + 1) * gw])
    y_ref[...] = jnp.concatenate(outs, axis=-1).astype(y_ref.dtype)


def _ssd(proj, small, params, *, L, nseq, ncs, row0, mix_width, init=None, mix=None):
    conv_w, conv_b, dtb, alog, dsk, nrm = params
    r = proj.shape[0]
    width = nrm.shape[1]
    rb0 = row0 // L
    rowblk = lambda s, c: rb0 + s * ncs + c
    const = lambda s, c: (0, 0)
    in_specs = [
        pl.BlockSpec((L, width), lambda s, c: (rowblk(s, c), 0)),
        pl.BlockSpec((L, width), lambda s, c: (rowblk(s, c), 1)),
        pl.BlockSpec((L, width), lambda s, c: (rowblk(s, c), 2)),
        pl.BlockSpec((L, 2 * LANE), lambda s, c: (rowblk(s, c), 0)),
    ]
    args = [proj, proj, proj, small]
    if init is not None:
        in_specs += [pl.BlockSpec((None, CONV_W - 1, 2 * width), lambda s, c: (s, 0, 0)),
                     pl.BlockSpec((None, width, SSD_STATE), lambda s, c: (s, 0, 0))]
        args += list(init)
    in_specs += [pl.BlockSpec((CONV_W, 2 * width), const), pl.BlockSpec((1, 2 * width), const),
                 pl.BlockSpec((1, LANE), const), pl.BlockSpec((1, LANE), const), pl.BlockSpec((1, LANE), const),
                 pl.BlockSpec((1, width), const)]
    args += [conv_w, conv_b, dtb, alog, dsk, nrm]
    aliases = {}
    if mix is not None:
        in_specs.append(pl.BlockSpec(memory_space=pl.ANY))
        aliases = {len(args): 0}
        args.append(mix)
    return pl.pallas_call(
        functools.partial(_ssd_kernel, has_init=init is not None, has_alias=mix is not None),
        out_shape=(jax.ShapeDtypeStruct((r, mix_width), BF16),
                   jax.ShapeDtypeStruct((nseq, width, SSD_STATE), F32),
                   jax.ShapeDtypeStruct((nseq, 8, 2 * width), F32)),
        grid=(nseq, ncs),
        in_specs=in_specs,
        out_specs=(
            pl.BlockSpec((L, width), lambda s, c: (rowblk(s, c), 0)),
            pl.BlockSpec((None, width, SSD_STATE), lambda s, c: (s, 0, 0)),
            pl.BlockSpec((None, 8, 2 * width), lambda s, c: (s, 0, 0)),
        ),
        scratch_shapes=[pltpu.VMEM((L + 8, 2 * width), F32), pltpu.VMEM((L, 2 * width), F32),
                        pltpu.VMEM((L, width), F32)],
        input_output_aliases=aliases,
        compiler_params=_cparams(("arbitrary", "arbitrary")),
        name="ssd",
    )(*args)


def _rope_pad(x, cos_t, sin_t):
    return x * cos_t + pltpu.roll(x, LANE // 2, axis=1) * sin_t


def _mla_q_kernel(cq_ref, ga_ref, w_ref, gn_ref, gr_ref, cos_ref, sin_ref, q_ref):
    cq = cq_ref[...]
    xn = cq * lax.rsqrt(jnp.mean(cq * cq, axis=-1, keepdims=True) + EPS) * ga_ref[...]
    q = _dot(xn.astype(BF16), w_ref[...])
    c, s = cos_ref[...], sin_ref[...]
    nope_w = MLA_HEADS * QK_NOPE
    for h in range(MLA_HEADS):
        qn = q[:, h * QK_NOPE:(h + 1) * QK_NOPE]
        qn = qn * lax.rsqrt(jnp.mean(qn * qn, axis=-1, keepdims=True) + EPS) * gn_ref[...]
        qp = q[:, nope_w + h * LANE:nope_w + (h + 1) * LANE]
        qp = qp * lax.rsqrt(jnp.sum(qp * qp, axis=-1, keepdims=True) * (1.0 / QK_ROPE) + EPS) * gr_ref[...]
        qp = _rope_pad(qp, c, s)
        q_ref[h] = (jnp.concatenate([qn, qp], axis=-1) * Q_PRESCALE).astype(q_ref.dtype)


def _mla_q(proj, cq_block, ga, w_uq_p, gn, gr_pad, cos_t, sin_t):
    r = proj.shape[0]
    bm = 512
    n = w_uq_p.shape[1]
    return pl.pallas_call(
        _mla_q_kernel,
        out_shape=jax.ShapeDtypeStruct((MLA_HEADS, r, 2 * LANE), BF16),
        grid=(r // bm,),
        in_specs=[
            pl.BlockSpec((bm, Q_LORA), lambda i: (i, cq_block)),
            pl.BlockSpec((1, Q_LORA), lambda i: (0, 0)),
            pl.BlockSpec((Q_LORA, n), lambda i: (0, 0)),
            pl.BlockSpec((1, QK_NOPE), lambda i: (0, 0)),
            pl.BlockSpec((1, LANE), lambda i: (0, 0)),
            pl.BlockSpec((bm, LANE), lambda i: (i, 0)),
            pl.BlockSpec((bm, LANE), lambda i: (i, 0)),
        ],
        out_specs=pl.BlockSpec((MLA_HEADS, bm, 2 * LANE), lambda i: (0, i, 0)),
        compiler_params=_cparams(("arbitrary",)),
        name="mla_q",
    )(proj, ga, w_uq_p, gn, gr_pad, cos_t, sin_t)


def _mla_ckv_kernel(ckv_ref, sm_ref, gkv_ref, gr_ref, cos_ref, sin_ref, ckv_out, kpe_out):
    x = ckv_ref[...]
    ckv_out[...] = x * lax.rsqrt(jnp.mean(x * x, axis=-1, keepdims=True) + EPS) * gkv_ref[...]
    kp = sm_ref[:, LANE:2 * LANE]
    kp = kp * lax.rsqrt(jnp.sum(kp * kp, axis=-1, keepdims=True) * (1.0 / QK_ROPE) + EPS) * gr_ref[...]
    kpe_out[...] = _rope_pad(kp, cos_ref[...], sin_ref[...])


def _mla_ckv(proj, small, ckv_block, gkv, gr_pad, cos_t, sin_t):
    r = proj.shape[0]
    bm = 512
    return pl.pallas_call(
        _mla_ckv_kernel,
        out_shape=(jax.ShapeDtypeStruct((r, KV_LORA), F32), jax.ShapeDtypeStruct((r, LANE), F32)),
        grid=(r // bm,),
        in_specs=[
            pl.BlockSpec((bm, KV_LORA), lambda i: (i, ckv_block)),
            pl.BlockSpec((bm, 2 * LANE), lambda i: (i, 0)),
            pl.BlockSpec((1, KV_LORA), lambda i: (0, 0)),
            pl.BlockSpec((1, LANE), lambda i: (0, 0)),
            pl.BlockSpec((bm, LANE), lambda i: (i, 0)),
            pl.BlockSpec((bm, LANE), lambda i: (i, 0)),
        ],
        out_specs=(pl.BlockSpec((bm, KV_LORA), lambda i: (i, 0)), pl.BlockSpec((bm, LANE), lambda i: (i, 0))),
        compiler_params=_cparams(("arbitrary",)),
        name="mla_ckv",
    )(proj, small, gkv, gr_pad, cos_t, sin_t)


def _mla_kv_new_kernel(ckv_ref, sm_ref, gkv_ref, gr_ref, cos_ref, sin_ref, w_ref, gk_ref,
                       ckv_out, kpe_out, k_ref, v_ref):
    x = ckv_ref[...]
    ckv = x * lax.rsqrt(jnp.mean(x * x, axis=-1, keepdims=True) + EPS) * gkv_ref[...]
    ckv_out[...] = ckv
    kp = sm_ref[:, LANE:2 * LANE]
    kp = kp * lax.rsqrt(jnp.sum(kp * kp, axis=-1, keepdims=True) * (1.0 / QK_ROPE) + EPS) * gr_ref[...]
    kpe = _rope_pad(kp, cos_ref[...], sin_ref[...])
    kpe_out[...] = kpe
    kv = _dot(ckv.astype(BF16), w_ref[...])
    nope_w = MLA_HEADS * QK_NOPE
    for h in range(MLA_HEADS):
        kn = kv[:, h * QK_NOPE:(h + 1) * QK_NOPE]
        kn = kn * lax.rsqrt(jnp.mean(kn * kn, axis=-1, keepdims=True) + EPS) * gk_ref[...]
        vh = kv[:, nope_w + h * V_DIM:nope_w + (h + 1) * V_DIM]
        k_ref[h] = jnp.concatenate([kn, kpe], axis=-1).astype(k_ref.dtype)
        v_ref[h] = jnp.concatenate([vh, jnp.ones_like(vh)], axis=-1).astype(v_ref.dtype)


def _mla_kv_new(proj, small, ckv_block, gkv, gr_pad, cos_t, sin_t, w_ukv_p, gk):
    r = proj.shape[0]
    bm = 512
    n = w_ukv_p.shape[1]
    row = lambda i: (i, 0)
    const = lambda i: (0, 0)
    return pl.pallas_call(
        _mla_kv_new_kernel,
        out_shape=(jax.ShapeDtypeStruct((r, KV_LORA), F32), jax.ShapeDtypeStruct((r, LANE), F32),
                   jax.ShapeDtypeStruct((MLA_HEADS, r, 2 * LANE), BF16),
                   jax.ShapeDtypeStruct((MLA_HEADS, r, 2 * V_DIM), BF16)),
        grid=(r // bm,),
        in_specs=[
            pl.BlockSpec((bm, KV_LORA), lambda i: (i, ckv_block)),
            pl.BlockSpec((bm, 2 * LANE), row),
            pl.BlockSpec((1, KV_LORA), const),
            pl.BlockSpec((1, LANE), const),
            pl.BlockSpec((bm, LANE), row),
            pl.BlockSpec((bm, LANE), row),
            pl.BlockSpec((KV_LORA, n), const),
            pl.BlockSpec((1, QK_NOPE), const),
        ],
        out_specs=(pl.BlockSpec((bm, KV_LORA), row), pl.BlockSpec((bm, LANE), row),
                   pl.BlockSpec((MLA_HEADS, bm, 2 * LANE), lambda i: (0, i, 0)),
                   pl.BlockSpec((MLA_HEADS, bm, 2 * V_DIM), lambda i: (0, i, 0))),
        compiler_params=_cparams(("arbitrary",)),
        name="mla_kv_new",
    )(proj, small, gkv, gr_pad, cos_t, sin_t, w_ukv_p, gk)


def _mla_kvup_kernel(*refs, compact):
    if compact:
        ckv_ref, w_ref, gk_ref, k_ref, v_ref = refs
    else:
        ckv_ref, kpe_ref, w_ref, gk_ref, k_ref, v_ref = refs
    kv = _dot(ckv_ref[...].astype(BF16), w_ref[...])
    nope_w = MLA_HEADS * QK_NOPE
    for h in range(MLA_HEADS):
        kn = kv[:, h * QK_NOPE:(h + 1) * QK_NOPE]
        kn = kn * lax.rsqrt(jnp.mean(kn * kn, axis=-1, keepdims=True) + EPS) * gk_ref[...]
        vh = kv[:, nope_w + h * V_DIM:nope_w + (h + 1) * V_DIM]
        if not compact:
            kn = jnp.concatenate([kn, kpe_ref[...]], axis=-1)
            vh = jnp.concatenate([vh, jnp.ones_like(vh)], axis=-1)
        k_ref[h] = kn.astype(k_ref.dtype)
        v_ref[h] = vh.astype(v_ref.dtype)


def _mla_kvup(ckv, kpe_pad, w_ukv_p, gk):
    r = ckv.shape[0]
    bm = 512
    n = w_ukv_p.shape[1]
    compact = kpe_pad is None
    kw, vw = (QK_NOPE, V_DIM) if compact else (2 * LANE, 2 * V_DIM)
    in_specs = [pl.BlockSpec((bm, KV_LORA), lambda i: (i, 0))]
    args = [ckv]
    if not compact:
        in_specs.append(pl.BlockSpec((bm, LANE), lambda i: (i, 0)))
        args.append(kpe_pad)
    in_specs += [pl.BlockSpec((KV_LORA, n), lambda i: (0, 0)), pl.BlockSpec((1, QK_NOPE), lambda i: (0, 0))]
    args += [w_ukv_p, gk]
    return pl.pallas_call(
        functools.partial(_mla_kvup_kernel, compact=compact),
        out_shape=(jax.ShapeDtypeStruct((MLA_HEADS, r, kw), BF16),
                   jax.ShapeDtypeStruct((MLA_HEADS, r, vw), BF16)),
        grid=(r // bm,),
        in_specs=in_specs,
        out_specs=(pl.BlockSpec((MLA_HEADS, bm, kw), lambda i: (0, i, 0)),
                   pl.BlockSpec((MLA_HEADS, bm, vw), lambda i: (0, i, 0))),
        compiler_params=_cparams(("arbitrary",)),
        name="mla_kvup",
    )(*args)


def _lane_fold(x, op):
    out = x[:, 0:LANE]
    for c in range(1, x.shape[1] // LANE):
        out = op(out, x[:, c * LANE:(c + 1) * LANE])
    return out


def _attn_prompt_kernel(q_ref, k_ref, v_ref, mix_ref, o_ref, s_scr, *, tq):
    del mix_ref
    tp = q_ref.shape[0]
    ri = lax.broadcasted_iota(jnp.int32, (tq, tq), 0) // CHUNK
    ci = lax.broadcasted_iota(jnp.int32, (tq, tq), 1) // CHUNK
    diag_visible = ci <= ri

    for qi in range(tp // tq):
        q = q_ref[qi * tq:(qi + 1) * tq, :]

        macc = None
        for kb in range(qi + 1):
            s = _dot_nt(q, k_ref[kb * tq:(kb + 1) * tq, :])
            if kb == qi:
                s = jnp.where(diag_visible, s, NEG_BIG)
            s_scr[kb] = s
            fold = _lane_fold(s, jnp.maximum)
            macc = fold if macc is None else jnp.maximum(macc, fold)
        m = jnp.max(macc, axis=-1, keepdims=True)

        acc = None
        for kb in range(qi + 1):
            p = jnp.exp2(s_scr[kb] - m)
            pv = _dot(p.astype(BF16), v_ref[kb * tq:(kb + 1) * tq, :])
            acc = pv if acc is None else acc + pv
        o_ref[qi * tq:(qi + 1) * tq, :] = (acc[:, :V_DIM] / acc[:, V_DIM:]).astype(o_ref.dtype)


def _attn_prompt(q, k, v, mix, *, bp, tp):
    tq = min(512, tp)
    r, mw = mix.shape
    col0 = (mw // 2) // V_DIM
    return pl.pallas_call(
        functools.partial(_attn_prompt_kernel, tq=tq),
        out_shape=jax.ShapeDtypeStruct((r, mw), mix.dtype),
        grid=(bp, MLA_HEADS),
        in_specs=[
            pl.BlockSpec((None, tp, 2 * LANE), lambda b, h: (h, b, 0)),
            pl.BlockSpec((None, tp, 2 * LANE), lambda b, h: (h, b, 0)),
            pl.BlockSpec((None, tp, 2 * V_DIM), lambda b, h: (h, b, 0)),
            pl.BlockSpec(memory_space=pl.ANY),
        ],
        out_specs=pl.BlockSpec((tp, V_DIM), lambda b, h: (b, col0 + h)),
        scratch_shapes=[pltpu.VMEM((tp // tq, tq, tq), F32)],
        input_output_aliases={3: 0},
        compiler_params=_cparams(("arbitrary", "arbitrary")),
        name="attn_prompt",
    )(q, k, v, mix)


def _attn_sample_kernel(q_ref, kp_ref, kpe_ref, vp_ref, kn_ref, vn_ref, mix_ref, o_ref, *, past):
    del mix_ref
    q = q_ref[...]
    ts = q.shape[0]
    k_past = jnp.concatenate([kp_ref[...], kpe_ref[...]], axis=-1)
    sp = _dot_nt(q, k_past)
    sn = _dot_nt(q, kn_ref[...])
    row = past + lax.broadcasted_iota(jnp.int32, (ts, ts), 0)
    col = past + lax.broadcasted_iota(jnp.int32, (ts, ts), 1)
    sn = jnp.where(col < (row // CHUNK + 1) * CHUNK, sn, NEG_BIG)
    m = jnp.maximum(jnp.max(sp, axis=-1, keepdims=True), jnp.max(sn, axis=-1, keepdims=True))
    pp = jnp.exp2(sp - m)
    pn = jnp.exp2(sn - m)
    l = jnp.sum(_lane_fold(pp, jnp.add), axis=-1, keepdims=True) + jnp.sum(pn, axis=-1, keepdims=True)
    o = _dot(pp.astype(BF16), vp_ref[...]) + _dot(pn.astype(BF16), vn_ref[:, :V_DIM])
    o_ref[...] = (o / l).astype(o_ref.dtype)


def _attn_sample(q, k_past, kpe_past, v_past, k_new, v_new, mix, *, bs, ts, past, row0):
    r, mw = mix.shape
    col0 = (mw // 2) // V_DIM
    g0 = row0 // ts
    return pl.pallas_call(
        functools.partial(_attn_sample_kernel, past=past),
        out_shape=jax.ShapeDtypeStruct((r, mw), mix.dtype),
        grid=(bs, MLA_HEADS),
        in_specs=[
            pl.BlockSpec((None, ts, 2 * LANE), lambda b, h: (h, g0 + b, 0)),
            pl.BlockSpec((None, past, QK_NOPE), lambda b, h: (h, b, 0)),
            pl.BlockSpec((past, LANE), lambda b, h: (b, 0)),
            pl.BlockSpec((None, past, V_DIM), lambda b, h: (h, b, 0)),
            pl.BlockSpec((None, ts, 2 * LANE), lambda b, h: (h, g0 + b, 0)),
            pl.BlockSpec((None, ts, 2 * V_DIM), lambda b, h: (h, g0 + b, 0)),
            pl.BlockSpec(memory_space=pl.ANY),
        ],
        out_specs=pl.BlockSpec((ts, V_DIM), lambda b, h: (g0 + b, col0 + h)),
        input_output_aliases={6: 0},
        compiler_params=_cparams(("arbitrary", "arbitrary")),
        name="attn_sample",
    )(q, k_past, kpe_past, v_past, k_new, v_new, mix)


def _ret_kernel(*refs, has_init, has_alias):
    refs = list(refs)
    q_ref, k_ref, v_ref, g_ref = refs[:4]
    del refs[:4]
    if has_init:
        s0_ref = refs.pop(0)
    dm_ref, qd_ref, kd_ref, cd_ref = refs[:4]
    del refs[:4]
    if has_alias:
        del refs[:1]
    o_ref, s_ref = refs
    hpg, kdim, vdim = s_ref.shape

    @pl.when(pl.program_id(2) == 0)
    def _():
        if has_init:
            s_ref[...] = s0_ref[...]
        else:
            s_ref[...] = jnp.zeros_like(s_ref)

    for h in range(hpg):
        q = q_ref[:, h * kdim:(h + 1) * kdim]
        k = k_ref[:, h * kdim:(h + 1) * kdim]
        v = v_ref[:, h * vdim:(h + 1) * vdim]
        s = s_ref[h]
        att = _dot_nt(q, k) * dm_ref[h]
        o = _dot(att.astype(BF16), v) + _dot(q, s.astype(BF16)) * qd_ref[:, h:h + 1]
        kdec = (k.astype(F32) * kd_ref[:, h:h + 1]).astype(BF16)
        s_ref[h] = s * cd_ref[:, h:h + 1] + _dot_tn(kdec, v)
        mu = jnp.mean(o, axis=-1, keepdims=True)
        oc = o - mu
        var = jnp.mean(oc * oc, axis=-1, keepdims=True)
        g = g_ref[:, h * vdim:(h + 1) * vdim].astype(F32)
        o_ref[:, h * vdim:(h + 1) * vdim] = (_silu(g) * (oc * lax.rsqrt(var + EPS))).astype(o_ref.dtype)


def _retention_tables(heads, hpg, L):
    lg = jnp.log1p(-jnp.exp2(-5.0 - jnp.arange(heads, dtype=F32)))
    idx = jnp.arange(L, dtype=F32)
    rel = idx[:, None] - idx[None, :]
    dmask = jnp.exp(jnp.where(rel[None] >= 0, rel[None] * lg[:, None, None], -jnp.inf))
    grp = lambda a: a.reshape(a.shape[0], heads // hpg, hpg).transpose(1, 0, 2)
    qdec = grp(jnp.exp((idx[:, None] + 1.0) * lg[None, :]))
    kdec = grp(jnp.exp((L - 1.0 - idx[:, None]) * lg[None, :]))
    cdec = grp(jnp.exp(L * lg)[None, :])
    return dmask, qdec, kdec, cdec


def _retention(qk, vg, *, L, nseq, ncs, row0, heads, s0=None, o_prev=None):
    r = qk.shape[0]
    kdim = qk.shape[1] // (2 * heads)
    vdim = vg.shape[1] // (2 * heads)
    hpg = 4
    nhg = heads // hpg
    dmask, qdec, kdec, cdec = _retention_tables(heads, hpg, L)
    rb0 = row0 // L
    rowblk = lambda s, c: rb0 + s * ncs + c
    in_specs = [
        pl.BlockSpec((L, hpg * kdim), lambda hg, s, c: (rowblk(s, c), hg)),
        pl.BlockSpec((L, hpg * kdim), lambda hg, s, c: (rowblk(s, c), nhg + hg)),
        pl.BlockSpec((L, hpg * vdim), lambda hg, s, c: (rowblk(s, c), hg)),
        pl.BlockSpec((L, hpg * vdim), lambda hg, s, c: (rowblk(s, c), nhg + hg)),
    ]
    args = [qk, qk, vg, vg]
    if s0 is not None:
        in_specs.append(pl.BlockSpec((None, hpg, kdim, vdim), lambda hg, s, c: (s, hg, 0, 0)))
        args.append(s0)
    in_specs += [pl.BlockSpec((hpg, L, L), lambda hg, s, c: (hg, 0, 0)),
                 pl.BlockSpec((None, L, hpg), lambda hg, s, c: (hg, 0, 0)),
                 pl.BlockSpec((None, L, hpg), lambda hg, s, c: (hg, 0, 0)),
                 pl.BlockSpec((None, 1, hpg), lambda hg, s, c: (hg, 0, 0))]
    args += [dmask, qdec, kdec, cdec]
    aliases = {}
    if o_prev is not None:
        in_specs.append(pl.BlockSpec(memory_space=pl.ANY))
        aliases = {len(args): 0}
        args.append(o_prev)
    return pl.pallas_call(
        functools.partial(_ret_kernel, has_init=s0 is not None, has_alias=o_prev is not None),
        out_shape=(jax.ShapeDtypeStruct((r, heads * vdim), BF16),
                   jax.ShapeDtypeStruct((nseq, heads, kdim, vdim), F32)),
        grid=(nhg, nseq, ncs),
        in_specs=in_specs,
        out_specs=(pl.BlockSpec((L, hpg * vdim), lambda hg, s, c: (rowblk(s, c), hg)),
                   pl.BlockSpec((None, hpg, kdim, vdim), lambda hg, s, c: (s, hg, 0, 0))),
        input_output_aliases=aliases,
        compiler_params=_cparams(("arbitrary", "arbitrary", "arbitrary")),
        name="retention",
    )(*args)


def _pad_rope_cols(a):
    half = QK_ROPE // 2
    z = jnp.zeros(a.shape[:-1] + (LANE // 2 - half,), a.dtype)
    return jnp.concatenate([a[..., :half], z, a[..., half:], z], axis=-1)


def _unpad_rope_cols(a):
    half = QK_ROPE // 2
    return jnp.concatenate([a[..., :half], a[..., LANE // 2:LANE // 2 + half]], axis=-1)


def _rope_tables(pos, half):
    inv = ROPE_THETA ** (-jnp.arange(half, dtype=F32) / half)
    ang = pos.astype(F32)[:, None] * inv[None, :]
    return jnp.cos(ang), jnp.sin(ang)


@jax.jit
def kernel(x_prompt, x_sample, c_prompt, c_sample, cache_mla_ckv, cache_mla_kpe, state_ssd, state_ssd_conv, state_ret, norm_mix, norm_ffn, w_ada, b_ada, w_in0, conv_w, conv_b, dt_bias, a_log, d_skip, ssd_norm, q_a_norm, w_uq, q_norm_nope, q_norm_rope, kv_a_norm, w_ukv, k_norm_nope, k_norm_rope, w_out0, w_in1, w_out1, w_gate, w_up, w_down):
    bp, tp, d = x_prompt.shape
    bs, ts, _ = x_sample.shape
    past = cache_mla_ckv.shape[2]
    assert ts == CHUNK and tp % CHUNK == 0 and past % CHUNK == 0
    rp, rs = bp * tp, bs * ts
    r = rp + rs
    ncp = tp // CHUNK
    bm = min(1024, r)
    assert r % bm == 0 and bm % 512 == 0

    ssd_width = d // 2
    ssd_heads = ssd_width // SSD_HEADDIM
    conv_ch = ssd_width + 2 * SSD_GROUPS * SSD_STATE
    off_dt = ssd_width + conv_ch
    off_cq = off_dt + ssd_heads
    off_ckv = off_cq + Q_LORA
    off_kpe = off_ckv + KV_LORA
    assert conv_ch == 2 * ssd_width and ssd_heads <= LANE
    mix0 = ssd_width + MLA_HEADS * V_DIM
    ret_kdim = d // RET_HEADS
    ret_qk = RET_HEADS * ret_kdim

    wi = w_in0[0]
    w0_main = jnp.concatenate([wi[:, :off_dt], wi[:, off_cq:off_kpe]], axis=1).astype(BF16)
    w0_small = jnp.concatenate([wi[:, off_dt:off_cq], jnp.zeros((d, LANE - ssd_heads), F32),
                                _pad_rope_cols(wi[:, off_kpe:])], axis=1).astype(BF16)
    wq = w_uq[0].reshape(Q_LORA, MLA_HEADS, QK_NOPE + QK_ROPE)
    w_uq_p = jnp.concatenate([wq[:, :, :QK_NOPE].reshape(Q_LORA, -1),
                              _pad_rope_cols(wq[:, :, QK_NOPE:]).reshape(Q_LORA, -1)], axis=1).astype(BF16)
    wkv = w_ukv[0].reshape(KV_LORA, MLA_HEADS, QK_NOPE + V_DIM)
    w_ukv_p = jnp.concatenate([wkv[:, :, :QK_NOPE].reshape(KV_LORA, -1),
                               wkv[:, :, QK_NOPE:].reshape(KV_LORA, -1)], axis=1).astype(BF16)
    w_out0_b = w_out0[0].astype(BF16)
    w1_b = w_in1[0].astype(BF16)
    w_out1_b = w_out1[0].astype(BF16)
    w_gate_b, w_up_b, w_down_b = w_gate.astype(BF16), w_up.astype(BF16), w_down.astype(BF16)

    def lane_pad(a, n=LANE):
        return jnp.pad(a, (0, n - a.shape[0])).reshape(1, n)

    pos = jnp.concatenate([jnp.tile(jnp.arange(tp), bp), jnp.tile(past + jnp.arange(ts), bs)])
    c32, s32 = _rope_tables(pos, QK_ROPE // 2)
    zq = jnp.zeros_like(c32)
    cos_m = jnp.concatenate([c32, zq, c32, zq], axis=1)
    sin_m = jnp.concatenate([-s32, zq, s32, zq], axis=1)
    cos_r, sin_r = _rope_tables(pos, ret_kdim // 2)
    lp = min(256, tp)

    nb = -(-(bp + bs) // 8) * 8
    c_all = jnp.concatenate([c_prompt, c_sample, jnp.zeros((nb - bp - bs, d), F32)], axis=0)
    mod = _ada(c_all, w_ada, b_ada)
    depth = mod.shape[0]
    mod = mod.reshape(depth, -1, 6, d).transpose(0, 2, 1, 3)
    mod_p = jnp.broadcast_to(mod[:, :, :bp, None, :], (depth, 6, bp, ncp, d)).reshape(depth, 6, bp * ncp, d)
    tbl = jnp.concatenate([mod_p, mod[:, :, bp:bp + bs]], axis=2)

    x_groups = ((x_prompt.reshape(rp, d), 0), (x_sample.reshape(rs, d), rp))
    bm_res = bm if rp % bm == 0 and rs % bm == 0 else 512
    assert rp % bm_res == 0 and rs % bm_res == 0
    bm_wide = r // 8
    assert bm_wide % 16 == 0

    def ffn(x, i, split_out=False):
        hn = _normmod(x, norm_ffn[i], tbl[i, 4], tbl[i, 3])
        d_ff = w_gate_b.shape[2]
        n_main = d_ff // 512 * 512
        gate_up = functools.partial(_mm, hn, [w_gate_b, w_up_b], layer=i, out_dtype=BF16, epi=_epi_swiglu,
                                    out_cols=d_ff, name="ffn_gate_up")
        hid = gate_up(n=n_main, bm=bm, bn=512)
        if n_main < d_ff:
            hid = gate_up(n=d_ff - n_main, w_col0=n_main, out_col0=n_main, bm=bm_wide, bn=256, prev=hid)
        down = functools.partial(_residual_mm, hid, w_down_b, x, tbl[i, 5], layer=i, bm=512, bn=512,
                                 name="ffn_down")
        if not split_out:
            return down()
        return [down(rows=n, x_row0=row0, res_row0=row0) for row0, n in ((0, rp), (rp, rs))]

    hn = None
    for x_part, row0 in x_groups:
        hn = _normmod(x_part, norm_mix[0], tbl[0, 1], tbl[0, 0], out_rows=r, row0=row0, prev=hn)
    proj = _mm(hn, [w0_main], bm=bm, bn=768, out_dtype=F32, epi=_epi_plain, name="in_proj0")
    small = _mm(hn, [w0_small], bm=bm, bn=2 * LANE, out_dtype=F32, epi=_epi_plain, name="in_proj0_small")

    ssd_params = (conv_w[0], conv_b[0].reshape(1, -1), lane_pad(dt_bias[0]), lane_pad(a_log[0]),
                  lane_pad(d_skip[0]), ssd_norm[0].reshape(1, -1))
    mix, h_p, conv_p8 = _ssd(proj, small, ssd_params, L=lp, nseq=bp, ncs=tp // lp, row0=0, mix_width=mix0)
    mix, h_s, conv_s8 = _ssd(proj, small, ssd_params, L=ts, nseq=bs, ncs=1, row0=rp, mix_width=mix0,
                             init=(state_ssd_conv[0], state_ssd[0].reshape(bs, ssd_width, SSD_STATE)), mix=mix)

    gr_q = _pad_rope_cols(q_norm_rope[0]).reshape(1, LANE)
    gr_k = _pad_rope_cols(k_norm_rope[0]).reshape(1, LANE)
    cq_block = (off_dt + 0) // Q_LORA
    ckv_block = (off_dt + Q_LORA) // KV_LORA
    q_all = _mla_q(proj, cq_block, q_a_norm[0].reshape(1, -1), w_uq_p, q_norm_nope[0].reshape(1, -1), gr_q,
                   cos_m, sin_m)
    gk = k_norm_nope[0].reshape(1, -1)
    ckv_all, kpe_all, k_all, v_all = _mla_kv_new(proj, small, ckv_block, kv_a_norm[0].reshape(1, -1), gr_k,
                                                 cos_m, sin_m, w_ukv_p, gk)
    k_past, v_past = _mla_kvup(cache_mla_ckv[0].reshape(bs * past, KV_LORA), None, w_ukv_p, gk)
    kpe_past = _pad_rope_cols(cache_mla_kpe[0].reshape(bs * past, QK_ROPE)).astype(BF16)
    mix = _attn_prompt(q_all, k_all, v_all, mix, bp=bp, tp=tp)
    mix = _attn_sample(q_all, k_past, kpe_past, v_past, k_all, v_all, mix, bs=bs, ts=ts, past=past, row0=rp)

    x = None
    for x_part, row0 in x_groups:
        x = _residual_mm(mix, w_out0_b, x_part, tbl[0, 2], bm=bm_res, bn=512, name="out_proj0",
                         rows=x_part.shape[0], x_row0=row0, out_rows=r, out_row0=row0, prev=x)
    x = ffn(x, 0)

    hn = _normmod(x, norm_mix[1], tbl[1, 1], tbl[1, 0])
    hpb = 1024 // ret_kdim
    rope_epi = functools.partial(_epi_rope_qk, head_dim=ret_kdim, k_block0=RET_HEADS // hpb,
                                 k_scale=ret_kdim ** -0.5)
    half = ret_kdim // 2
    qk = _mm(hn, [w1_b], n=2 * ret_qk, bm=bm, bn=1024, out_dtype=BF16, epi=rope_epi,
             extras=[(cos_r, (bm, half), lambda i, j: (i, 0)), (sin_r, (bm, half), lambda i, j: (i, 0))],
             name="in_proj1_qk")
    vg = _mm(hn, [w1_b], n=w1_b.shape[1] - 2 * ret_qk, w_col0=2 * ret_qk, bm=bm, bn=1024, out_dtype=BF16,
             epi=_epi_plain, name="in_proj1_vg")
    o_ret, ret_p = _retention(qk, vg, L=lp, nseq=bp, ncs=tp // lp, row0=0, heads=RET_HEADS)
    o_ret, ret_s = _retention(qk, vg, L=ts, nseq=bs, ncs=1, row0=rp, heads=RET_HEADS, s0=state_ret[0],
                              o_prev=o_ret)
    x = _residual_mm(o_ret, w_out1_b, x, tbl[1, 2], bm=bm, bn=1024, bk=2048, name="out_proj1")
    y_p, y_s = ffn(x, 1, split_out=True)

    def split(a, shape_p, shape_s, n=rp):
        return a[:n].reshape(shape_p), a[n:].reshape(shape_s)

    y_p, y_s = y_p.reshape(bp, tp, d), y_s.reshape(bs, ts, d)
    ckv_p, ckv_s = split(ckv_all, (1, bp, tp, KV_LORA), (1, bs, ts, KV_LORA))
    kpe_p, kpe_s = split(_unpad_rope_cols(kpe_all), (1, bp, tp, QK_ROPE), (1, bs, ts, QK_ROPE))
    hshape = (ssd_heads, SSD_HEADDIM, SSD_STATE)
    ssd_p, ssd_s = h_p.reshape((1, bp) + hshape), h_s.reshape((1, bs) + hshape)
    conv_p, conv_s = conv_p8[None, :, 8 - (CONV_W - 1):], conv_s8[None, :, 8 - (CONV_W - 1):]
    return (y_p, y_s, ckv_p, kpe_p, ssd_p, conv_p, ret_p[None], ckv_s, kpe_s, ssd_s, conv_s, ret_s[None])
```
